```python
import math
import jax, jax.numpy as jnp
from jax import lax
import numpy as np

D_MODEL = 2048
BATCH = 2
SEQ = 4096
DEPTH = 4
DEC_BATCH = 8
DEC_SEQ = 8
PAST_LEN = 16384
PAGE_SIZE = 128

CONF_W = 512
CONF_K = 31
DN_HEADS = 4
DN_DK = 128
DN_DV = 128
DN_CONV = 4
DN_CHUNK = 64
DN_QKV = DN_HEADS * (2 * DN_DK + DN_DV)
DIFF_HEADS = 8
DIFF_DQK = 64
DIFF_DV = 2 * DIFF_DQK
ROT_DIM = DIFF_DQK // 4
ROPE_THETA = 500000.0
Q_BLOCK = 128
MIX_W = CONF_W + DN_HEADS * DN_DV + DIFF_HEADS * DIFF_DV
IN_SIZES = (2 * CONF_W, DN_QKV, DN_HEADS * DN_DV, DN_HEADS, DN_HEADS,
            2 * DIFF_HEADS * DIFF_DQK, 2 * DIFF_HEADS * DIFF_DQK, DIFF_HEADS * DIFF_DV)
IN_W = sum(IN_SIZES)
N_GROUPS = 4
EXPERTS_PER_GROUP = 4
N_EXPERTS = N_GROUPS * EXPERTS_PER_GROUP
TOP_K_IN_GROUP = 2
D_EXPERT = 512
EPS = 1e-6

kernel_name = 'hymba_conv_gdn_diffattn_hmoe_step'


def rmsnorm(x, g):
    xf = x.astype(jnp.float32)
    y = xf * lax.rsqrt(jnp.mean(xf * xf, axis=-1, keepdims=True) + EPS)
    return (y * g.astype(jnp.float32)).astype(x.dtype)


def layernorm(x, g, b):
    xf = x.astype(jnp.float32)
    mu = jnp.mean(xf, axis=-1, keepdims=True)
    xc = xf - mu
    y = xc * lax.rsqrt(jnp.mean(xc * xc, axis=-1, keepdims=True) + EPS)
    return (y * g.astype(jnp.float32) + b.astype(jnp.float32)).astype(x.dtype)


def l2norm(x):
    xf = x.astype(jnp.float32)
    return xf * lax.rsqrt(jnp.sum(xf * xf, axis=-1, keepdims=True) + EPS)


def causal_dwconv(x, prev, w):
    xp = jnp.concatenate([prev.astype(x.dtype), x], axis=1)
    y = lax.conv_general_dilated(xp, w[:, None, :].astype(x.dtype), window_strides=(1,), padding='VALID',
                                 dimension_numbers=('NWC', 'WIO', 'NWC'), feature_group_count=x.shape[-1])
    return y, xp[:, xp.shape[1] - (w.shape[0] - 1):]


def rope(x, pos):
    half = ROT_DIM // 2
    inv = ROPE_THETA ** (-jnp.arange(0, ROT_DIM, 2, dtype=jnp.float32) / ROT_DIM)
    ang = pos.astype(jnp.float32)[:, None] * inv[None, :]
    cos = jnp.cos(ang)[None, :, None, :]
    sin = jnp.sin(ang)[None, :, None, :]
    xr = x[..., :ROT_DIM].astype(jnp.float32)
    x1, x2 = xr[..., :half], xr[..., half:]
    rot = jnp.concatenate([x1 * cos - x2 * sin, x2 * cos + x1 * sin], axis=-1)
    return jnp.concatenate([rot.astype(x.dtype), x[..., ROT_DIM:]], axis=-1)


def diff_attend(q, k, v, q_pos, k_pos, lam):
    B, Tq = q.shape[0], q.shape[1]
    Tk = k.shape[1]
    s = jnp.einsum('bqhd,bkhd->bhqk', q, k).astype(jnp.float32) * (DIFF_DQK ** -0.5)
    s = jnp.where(k_pos[None, :] <= q_pos[:, None], s, -jnp.inf)
    p = jax.nn.softmax(s, axis=-1).reshape(B, DIFF_HEADS, 2, Tq, Tk)
    w = p[:, :, 0] - lam * p[:, :, 1]
    return jnp.einsum('bhqk,bkhd->bqhd', w.astype(v.dtype), v)


def diff_attend_blocked(q, k, v, lam):
    B, T = q.shape[0], q.shape[1]
    qblk = Q_BLOCK if T % Q_BLOCK == 0 else T
    nb = T // qblk
    qb = jnp.swapaxes(q.reshape(B, nb, qblk, q.shape[2], q.shape[3]), 0, 1)
    k_pos = jnp.arange(T, dtype=jnp.int32)

    def one_block(args):
        qi, i = args
        q_pos = i * qblk + jnp.arange(qblk, dtype=jnp.int32)
        return diff_attend(qi, k, v, q_pos, k_pos, lam)

    out = lax.map(one_block, (qb, jnp.arange(nb, dtype=jnp.int32)))
    return jnp.swapaxes(out, 0, 1).reshape(B, T, out.shape[3], out.shape[4])


def chunk_gated_delta(q, k, v, g, beta, s0):
    f32 = jnp.float32
    B, T, H, dk = k.shape
    dv = v.shape[-1]
    C = DN_CHUNK
    pad = (-T) % C
    N = (T + pad) // C

    def to_chunks(a):
        a = jnp.pad(a.astype(f32), [(0, 0), (0, pad)] + [(0, 0)] * (a.ndim - 2))
        a = a.reshape((B, N, C) + a.shape[2:])
        return jnp.moveaxis(a, (1, 3), (0, 2))

    qc, kc, vc, gc, bc = [to_chunks(a) for a in (q, k, v, g, beta)]
    gc = jnp.cumsum(gc, axis=-1)
    idx = jnp.arange(C)
    tril = idx[:, None] >= idx[None, :]
    decay = jnp.exp(jnp.where(tril, gc[..., :, None] - gc[..., None, :], -jnp.inf))
    kb = kc * bc[..., None]
    lower = jnp.einsum('...id,...jd->...ij', kb, kc) * decay
    a_mat = jnp.where(idx[:, None] > idx[None, :], lower, 0.0) + jnp.eye(C, dtype=f32)
    rhs = jnp.concatenate([vc * bc[..., None], kb * jnp.exp(gc)[..., None]], axis=-1)
    sol = lax.linalg.triangular_solve(a_mat, rhs, left_side=True, lower=True, unit_diagonal=True)
    u, w = sol[..., :dv], sol[..., dv:]
    attn = jnp.einsum('...id,...jd->...ij', qc, kc) * decay
    q_dec = qc * jnp.exp(gc)[..., None]
    g_last = gc[..., -1]
    k_dec = kc * jnp.exp(g_last[..., None] - gc)[..., None]

    def step(S, xs):
        u_i, w_i, a_i, qd_i, kd_i, gl_i = xs
        v_new = u_i - jnp.einsum('bhck,bhkv->bhcv', w_i, S)
        o = jnp.einsum('bhck,bhkv->bhcv', qd_i, S) + jnp.einsum('bhij,bhjv->bhiv', a_i, v_new)
        S = S * jnp.exp(gl_i)[..., None, None] + jnp.einsum('bhck,bhcv->bhkv', kd_i, v_new)
        return S, o

    S, o = lax.scan(step, s0.astype(f32), (u, w, attn, q_dec, k_dec, g_last))
    o = jnp.moveaxis(o, (0, 2), (1, 3)).reshape(B, N * C, H, dv)[:, :T]
    return o, S


def hier_moe(x, rg_w, rg_b, re_w, re_b, w_gate, w_up, w_down):
    f32 = jnp.float32
    B, T, D = x.shape
    xt = x.reshape(B * T, D)
    n = xt.shape[0]
    glog = (xt @ rg_w).astype(f32) + rg_b.astype(f32)
    gsel = jnp.argmax(glog, axis=-1)
    gw = jnp.take_along_axis(jax.nn.softmax(glog, axis=-1), gsel[:, None], axis=1)
    elog = ((xt @ re_w).astype(f32) + re_b.astype(f32)).reshape(n, N_GROUPS, EXPERTS_PER_GROUP)
    elog = jnp.take_along_axis(elog, gsel[:, None, None], axis=1)[:, 0]
    top_v, top_i = lax.top_k(elog, TOP_K_IN_GROUP)
    ew = jax.nn.softmax(top_v, axis=-1) * gw
    eid = gsel[:, None] * EXPERTS_PER_GROUP + top_i
    gates = jnp.sum(jax.nn.one_hot(eid, N_EXPERTS, dtype=f32) * ew[..., None], axis=1)
    hg = jnp.einsum('nd,edf->nef', xt, w_gate)
    hu = jnp.einsum('nd,edf->nef', xt, w_up)
    act = jax.nn.silu(hg) * hu * gates[..., None].astype(x.dtype)
    return jnp.einsum('nef,efd->nd', act, w_down).reshape(B, T, D)


def layer(x, pos, conf_prev, dnc_prev, dn_s0, past_k, past_v, lp, lam_init):
    f32 = jnp.float32
    B, T, _ = x.shape
    h = rmsnorm(x, lp['norm1_g'])
    split_at = np.cumsum(IN_SIZES)[:-1].tolist()
    glu_in, dn_qkv, dn_z, dn_b, dn_a, dq, dk, dvv = jnp.split(h @ lp['w_in'], split_at, axis=-1)

    val, gate = jnp.split(glu_in, 2, axis=-1)
    c, conf_new = causal_dwconv(val * jax.nn.sigmoid(gate), conf_prev, lp['conf_dw_w'])
    conf_out = jax.nn.silu(layernorm(c + lp['conf_dw_b'], lp['conf_ln_g'], lp['conf_ln_b']))

    qkv, dnc_new = causal_dwconv(dn_qkv, dnc_prev, lp['dn_conv_w'])
    qkv = jax.nn.silu(qkv)
    q_dn, k_dn, v_dn = jnp.split(qkv, [DN_HEADS * DN_DK, 2 * DN_HEADS * DN_DK], axis=-1)
    q_dn = l2norm(q_dn.reshape(B, T, DN_HEADS, DN_DK)) * (DN_DK ** -0.5)
    k_dn = l2norm(k_dn.reshape(B, T, DN_HEADS, DN_DK))
    beta = jax.nn.sigmoid(dn_b.astype(f32))
    g = -jnp.exp(lp['dn_a_log'].astype(f32)) * jax.nn.softplus(dn_a.astype(f32) + lp['dn_dt_bias'].astype(f32))
    o_dn, dn_state = chunk_gated_delta(q_dn, k_dn, v_dn.reshape(B, T, DN_HEADS, DN_DV), g, beta, dn_s0)
    o_dn = rmsnorm(o_dn.astype(x.dtype), lp['dn_norm_g']) * jax.nn.silu(dn_z.reshape(B, T, DN_HEADS, DN_DV))

    q = rope(rmsnorm(dq.reshape(B, T, 2 * DIFF_HEADS, DIFF_DQK), lp['diff_q_norm_g']), pos)
    k = rope(rmsnorm(dk.reshape(B, T, 2 * DIFF_HEADS, DIFF_DQK), lp['diff_k_norm_g']), pos)
    v = dvv.reshape(B, T, DIFF_HEADS, DIFF_DV)
    lam = (jnp.exp(jnp.sum(lp['lq1'].astype(f32) * lp['lk1'].astype(f32)))
           - jnp.exp(jnp.sum(lp['lq2'].astype(f32) * lp['lk2'].astype(f32))) + lam_init)
    if past_k is None:
        att = diff_attend_blocked(q, k, v, lam)
    else:
        kk = jnp.concatenate([past_k.astype(k.dtype), k], axis=1)
        vv = jnp.concatenate([past_v.astype(v.dtype), v], axis=1)
        att = diff_attend(q, kk, vv, pos, jnp.arange(kk.shape[1], dtype=jnp.int32), lam)
    att = rmsnorm(att, lp['diff_subln_g']) * (1.0 - lam_init)

    mixed = jnp.concatenate([conf_out, o_dn.reshape(B, T, -1), att.reshape(B, T, -1).astype(x.dtype)], axis=-1)
    x = x + mixed @ lp['w_out']
    x = x + hier_moe(rmsnorm(x, lp['norm2_g']), lp['rg_w'], lp['rg_b'], lp['re_w'], lp['re_b'],
                     lp['w_gate'], lp['w_up'], lp['w_down'])
    return x, (k, v, dn_state.astype(x.dtype), dnc_new, conf_new)


def setup_inputs(seed: int = 0) -> dict:
    key = jax.random.key(seed)
    ks = iter(jax.random.split(key, 40))
    f32 = jnp.float32

    def nrm(shape, scale):
        return scale * jax.random.normal(next(ks), shape, f32)

    n_pages = PAST_LEN // PAGE_SIZE
    n_used = DEC_BATCH * n_pages
    n_pool = n_used + max(1, n_used // 4)
    x_prompt = nrm((BATCH, SEQ, D_MODEL), 1.0)
    x_sample = nrm((DEC_BATCH, DEC_SEQ, D_MODEL), 1.0)
    cache_k = nrm((DEPTH, n_pool, PAGE_SIZE, 2 * DIFF_HEADS, DIFF_DQK), 1.0)
    cache_v = nrm((DEPTH, n_pool, PAGE_SIZE, DIFF_HEADS, DIFF_DV), 1.0)
    page_table = jax.random.permutation(next(ks), n_pool)[:n_used].reshape(DEC_BATCH, n_pages).astype(jnp.int32)
    state_delta = nrm((DEPTH, DEC_BATCH, DN_HEADS, DN_DK, DN_DV), 0.5)
    state_delta_conv = nrm((DEPTH, DEC_BATCH, DN_CONV - 1, DN_QKV), 1.0)
    state_conf_conv = nrm((DEPTH, DEC_BATCH, CONF_K - 1, CONF_W), 0.5)
    dt = jnp.exp(jax.random.uniform(next(ks), (DEPTH, DN_HEADS), f32, math.log(1e-3), math.log(1e-1)))
    return {
        'x_prompt': x_prompt,
        'x_sample': x_sample,
        'cache_k': cache_k,
        'cache_v': cache_v,
        'page_table': page_table,
        'state_delta': state_delta,
        'state_delta_conv': state_delta_conv,
        'state_conf_conv': state_conf_conv,
        'norm1_g': 1.0 + nrm((DEPTH, D_MODEL), 0.02),
        'w_in': nrm((DEPTH, D_MODEL, IN_W), D_MODEL ** -0.5),
        'conf_dw_w': nrm((DEPTH, CONF_K, CONF_W), CONF_K ** -0.5),
        'conf_dw_b': nrm((DEPTH, CONF_W), 0.02),
        'conf_ln_g': 1.0 + nrm((DEPTH, CONF_W), 0.02),
        'conf_ln_b': nrm((DEPTH, CONF_W), 0.02),
        'dn_conv_w': nrm((DEPTH, DN_CONV, DN_QKV), DN_CONV ** -0.5),
        'dn_a_log': jnp.log(jax.random.uniform(next(ks), (DEPTH, DN_HEADS), f32, 1.0, 16.0)),
        'dn_dt_bias': dt + jnp.log(-jnp.expm1(-dt)),
        'dn_norm_g': 1.0 + nrm((DEPTH, DN_DV), 0.02),
        'diff_q_norm_g': 1.0 + nrm((DEPTH, DIFF_DQK), 0.02),
        'diff_k_norm_g': 1.0 + nrm((DEPTH, DIFF_DQK), 0.02),
        'diff_lambda_q1': nrm((DEPTH, DIFF_DQK), 0.1),
        'diff_lambda_k1': nrm((DEPTH, DIFF_DQK), 0.1),
        'diff_lambda_q2': nrm((DEPTH, DIFF_DQK), 0.1),
        'diff_lambda_k2': nrm((DEPTH, DIFF_DQK), 0.1),
        'diff_subln_g': 1.0 + nrm((DEPTH, DIFF_DV), 0.02),
        'w_out': nrm((DEPTH, MIX_W, D_MODEL), 0.5 * MIX_W ** -0.5),
        'norm2_g': 1.0 + nrm((DEPTH, D_MODEL), 0.02),
        'router_group_w': nrm((DEPTH, D_MODEL, N_GROUPS), D_MODEL ** -0.5),
        'router_group_b': nrm((DEPTH, N_GROUPS), 0.01),
        'router_expert_w': nrm((DEPTH, D_MODEL, N_EXPERTS), D_MODEL ** -0.5),
        'router_expert_b': nrm((DEPTH, N_EXPERTS), 0.01),
        'moe_w_gate': nrm((DEPTH, N_EXPERTS, D_MODEL, D_EXPERT), D_MODEL ** -0.5),
        'moe_w_up': nrm((DEPTH, N_EXPERTS, D_MODEL, D_EXPERT), D_MODEL ** -0.5),
        'moe_w_down': nrm((DEPTH, N_EXPERTS, D_EXPERT, D_MODEL), 0.5 * D_EXPERT ** -0.5),
    }


def reference(x_prompt, x_sample, cache_k, cache_v, page_table, state_delta, state_delta_conv,
              state_conf_conv, norm1_g, w_in, conf_dw_w, conf_dw_b, conf_ln_g, conf_ln_b, dn_conv_w,
              dn_a_log, dn_dt_bias, dn_norm_g, diff_q_norm_g, diff_k_norm_g, diff_lambda_q1,
              diff_lambda_k1, diff_lambda_q2, diff_lambda_k2, diff_subln_g, w_out, norm2_g,
              router_group_w, router_group_b, router_expert_w, router_expert_b, moe_w_gate,
              moe_w_up, moe_w_down):
    B, S, _ = x_prompt.shape
    DB, T, _ = x_sample.shape
    n_pages = PAST_LEN // PAGE_SIZE
    pos_p = jnp.arange(S, dtype=jnp.int32)
    pos_s = PAST_LEN + jnp.arange(T, dtype=jnp.int32)
    hp, hs = x_prompt, x_sample
    kp_l, vp_l, ks_l, vs_l = [], [], [], []
    dp_l, ds_l, dcp_l, dcs_l, ccp_l, ccs_l = [], [], [], [], [], []
    for l in range(DEPTH):
        lp = {'norm1_g': norm1_g[l], 'w_in': w_in[l], 'conf_dw_w': conf_dw_w[l], 'conf_dw_b': conf_dw_b[l],
              'conf_ln_g': conf_ln_g[l], 'conf_ln_b': conf_ln_b[l], 'dn_conv_w': dn_conv_w[l],
              'dn_a_log': dn_a_log[l], 'dn_dt_bias': dn_dt_bias[l], 'dn_norm_g': dn_norm_g[l],
              'diff_q_norm_g': diff_q_norm_g[l], 'diff_k_norm_g': diff_k_norm_g[l],
              'lq1': diff_lambda_q1[l], 'lk1': diff_lambda_k1[l], 'lq2': diff_lambda_q2[l],
              'lk2': diff_lambda_k2[l], 'diff_subln_g': diff_subln_g[l], 'w_out': w_out[l],
              'norm2_g': norm2_g[l], 'rg_w': router_group_w[l], 'rg_b': router_group_b[l],
              're_w': router_expert_w[l], 're_b': router_expert_b[l], 'w_gate': moe_w_gate[l],
              'w_up': moe_w_up[l], 'w_down': moe_w_down[l]}
        lam_init = 0.8 - 0.6 * math.exp(-0.3 * l)
        hp, (kp, vp, dsp, dcp, ccp) = layer(
            hp, pos_p,
            jnp.zeros((B, CONF_K - 1, CONF_W), hp.dtype),
            jnp.zeros((B, DN_CONV - 1, DN_QKV), hp.dtype),
            jnp.zeros((B, DN_HEADS, DN_DK, DN_DV), hp.dtype),
            None, None, lp, lam_init)
        past_k = cache_k[l, page_table].reshape(DB, n_pages * PAGE_SIZE, 2 * DIFF_HEADS, DIFF_DQK)
        past_v = cache_v[l, page_table].reshape(DB, n_pages * PAGE_SIZE, DIFF_HEADS, DIFF_DV)
        hs, (ksm, vsm, dss, dcs, ccs) = layer(
            hs, pos_s, state_conf_conv[l], state_delta_conv[l], state_delta[l],
            past_k, past_v, lp, lam_init)
        kp_l.append(kp); vp_l.append(vp); ks_l.append(ksm); vs_l.append(vsm)
        dp_l.append(dsp); ds_l.append(dss); dcp_l.append(dcp); dcs_l.append(dcs)
        ccp_l.append(ccp); ccs_l.append(ccs)
    return (hp, hs, jnp.stack(kp_l), jnp.stack(vp_l), jnp.stack(ks_l), jnp.stack(vs_l),
            jnp.stack(dp_l), jnp.stack(ds_l), jnp.stack(dcp_l), jnp.stack(dcs_l),
            jnp.stack(ccp_l), jnp.stack(ccs_l))
```

```python
import functools
import math

import jax
import jax.numpy as jnp
from jax import lax
from jax.experimental import pallas as pl
from jax.experimental.pallas import tpu as pltpu

F32 = jnp.float32
BF16 = jnp.bfloat16
SDS = jax.ShapeDtypeStruct

D_MODEL = 2048
CONF_W = 512
CONF_K = 31
DN_HEADS = 4
DN_DK = 128
DN_DV = 128
DN_CONV = 4
DN_CHUNK = 64
DN_QKV = DN_HEADS * (2 * DN_DK + DN_DV)
DIFF_HEADS = 8
DIFF_DQK = 64
DIFF_DV = 128
ROT_DIM = 16
ROPE_THETA = 500000.0
N_GROUPS = 4
EXPERTS_PER_GROUP = 4
N_EXPERTS = 16
D_EXPERT = 512
EPS = 1e-6
LANES = 128
VMEM_LIMIT = 48 * 1024 * 1024

P_QKV, P_Z, P_GLU, P_DQ, P_DK, P_DV = 0, 1536, 2048, 3072, 4096, 5120
P_W = 6144
NEG = -1e30


def _cparams(*sem):
    return pltpu.CompilerParams(dimension_semantics=sem, vmem_limit_bytes=VMEM_LIMIT)


def _dot(a, b):
    return jnp.dot(a, b, preferred_element_type=F32)


def _dot_nt(a, b):
    return lax.dot_general(a, b, (((1,), (1,)), ((), ())), preferred_element_type=F32)


def _split(a):
    hi = a.astype(BF16)
    return hi, (a - hi.astype(F32)).astype(BF16)


def _mm3(a, b):
    ah, al = _split(a)
    bh, bl = _split(b)
    return _dot(ah, bh) + (_dot(ah, bl) + _dot(al, bh))


def _sigmoid(x):
    return 1.0 / (1.0 + jnp.exp(-x))


def _silu(x):
    return x * _sigmoid(x)


def _softplus(x):
    return jnp.maximum(x, 0.0) + jnp.log1p(jnp.exp(-jnp.abs(x)))


def _inproj_kernel(x_ref, g_ref, w_ref, wba_ref, o_ref, oba_ref, h_ref):
    @pl.when(pl.program_id(1) == 0)
    def _():
        x = x_ref[...]
        y = x * lax.rsqrt(jnp.mean(x * x, axis=-1, keepdims=True) + EPS)
        h_ref[...] = (y * g_ref[...]).astype(BF16)
        oba_ref[...] = _dot(h_ref[...], wba_ref[...])

    o_ref[...] = _dot(h_ref[...], w_ref[...])


def _in_proj(x, g, w_main, w_ba):
    n = x.shape[0]
    tm = min(n, 1024)
    tn = 512
    return pl.pallas_call(
        _inproj_kernel,
        grid=(n // tm, P_W // tn),
        in_specs=[pl.BlockSpec((tm, D_MODEL), lambda i, j: (i, 0)),
                  pl.BlockSpec((1, D_MODEL), lambda i, j: (0, 0)),
                  pl.BlockSpec((D_MODEL, tn), lambda i, j: (0, j)),
                  pl.BlockSpec((D_MODEL, LANES), lambda i, j: (0, 0))],
        out_specs=[pl.BlockSpec((tm, tn), lambda i, j: (i, j)),
                   pl.BlockSpec((tm, LANES), lambda i, j: (i, 0))],
        out_shape=[SDS((n, P_W), F32), SDS((n, LANES), F32)],
        scratch_shapes=[pltpu.VMEM((tm, D_MODEL), BF16)],
        compiler_params=_cparams("parallel", "arbitrary"),
        name="in_proj",
    )(x, g, w_main, w_ba)


def _conf_kernel(p_ref, prev_ref, w_ref, b_ref, g_ref, bb_ref, o_ref, new_ref, ubuf, *, tt, nt):
    t = pl.program_id(1)
    hist = 32

    @pl.when(t == 0)
    def _():
        ubuf[0:2, :] = jnp.zeros((2, CONF_W), F32)
        ubuf[2:hist, :] = prev_ref[0]

    if nt > 1:
        @pl.when(t > 0)
        def _():
            ubuf[0:hist, :] = ubuf[tt:tt + hist, :]

    x = p_ref[...]
    ubuf[hist:hist + tt, :] = x[:, :CONF_W] * _sigmoid(x[:, CONF_W:])
    rc = min(tt, 64)
    for r0 in range(0, tt, rc):
        acc = jnp.zeros((rc, CONF_W), F32)
        for j in range(CONF_K):
            s = r0 + hist - (CONF_K - 1) + j
            acc = acc + w_ref[j:j + 1, :] * ubuf[s:s + rc, :]
        c = acc + b_ref[...]
        xc = c - jnp.mean(c, axis=-1, keepdims=True)
        y = xc * lax.rsqrt(jnp.mean(xc * xc, axis=-1, keepdims=True) + EPS)
        y = y * g_ref[...] + bb_ref[...]
        o_ref[r0:r0 + rc, :] = _silu(y).astype(o_ref.dtype)

    @pl.when(t == nt - 1)
    def _():
        new_ref[0] = ubuf[tt + hist - (CONF_K - 1):tt + hist, :]


def _conformer(p, prev, w, b, g, bb, B, T, act_dtype):
    tt = min(T, 256)
    nt = T // tt
    return pl.pallas_call(
        functools.partial(_conf_kernel, tt=tt, nt=nt),
        grid=(B, nt),
        in_specs=[pl.BlockSpec((tt, 2 * CONF_W), lambda b_, t: (b_ * nt + t, P_GLU // (2 * CONF_W))),
                  pl.BlockSpec((1, CONF_K - 1, CONF_W), lambda b_, t: (b_, 0, 0)),
                  pl.BlockSpec((CONF_K, CONF_W), lambda b_, t: (0, 0)),
                  pl.BlockSpec((1, CONF_W), lambda b_, t: (0, 0)),
                  pl.BlockSpec((1, CONF_W), lambda b_, t: (0, 0)),
                  pl.BlockSpec((1, CONF_W), lambda b_, t: (0, 0))],
        out_specs=[pl.BlockSpec((tt, CONF_W), lambda b_, t: (b_ * nt + t, 0)),
                   pl.BlockSpec((1, CONF_K - 1, CONF_W), lambda b_, t: (b_, 0, 0))],
        out_shape=[SDS((B * T, CONF_W), act_dtype), SDS((B, CONF_K - 1, CONF_W), F32)],
        scratch_shapes=[pltpu.VMEM((tt + 32, CONF_W), F32)],
        compiler_params=_cparams("parallel", "arbitrary"),
        name="conformer",
    )(p, prev, w, b, g, bb)


def _stack_heads(a):
    return jnp.concatenate([a[:, h * LANES:(h + 1) * LANES] for h in range(DN_HEADS)], axis=0)


def _unstack_heads(a, c):
    return jnp.concatenate([a[h * c:(h + 1) * c, :] for h in range(DN_HEADS)], axis=1)


def _stack_cols(a, lane0):
    return jnp.concatenate([a[:, lane0 + h:lane0 + h + 1] for h in range(DN_HEADS)], axis=0)


def _diag_blocks(x, rowhead):
    out = jnp.where(rowhead == 0, x[:, 0:LANES], 0.0)
    for h in range(1, DN_HEADS):
        out = out + jnp.where(rowhead == h, x[:, h * LANES:(h + 1) * LANES], 0.0)
    return out


def _unit_lower_inverse(lmat, ri, ci):
    n = lmat.shape[0]
    eye = (ri == ci).astype(F32)
    d = jnp.where((ri // 16) == (ci // 16), lmat, 0.0)
    d2 = _mm3(d, d)
    d4 = _mm3(d2, d2)
    d8 = _mm3(d4, d4)
    t = _mm3(_mm3(_mm3(eye - d, eye + d2), eye + d4), eye + d8)
    m32 = jnp.where(((ri // 32) == (ci // 32)) & ((ri // 16) != (ci // 16)), lmat, 0.0)
    t = t - _mm3(_mm3(t, m32), t)
    m64 = jnp.where((ri // 32) != (ci // 32), lmat, 0.0)
    t = t - _mm3(_mm3(t, m64), t)
    del n
    return t


def _dn_kernel(qkv_ref, z_ref, ba_ref, prev_ref, cw_ref, av_ref, dv_ref, ng_ref, s0_ref,
               o_ref, cnew_ref, snew_ref, cbuf, s_sc, *, tb, nb):
    t = pl.program_id(1)
    C = DN_CHUNK
    H = DN_HEADS
    HC = H * C
    npre = DN_CONV - 1

    @pl.when(t == 0)
    def _():
        cbuf[0:8 - npre, :] = jnp.zeros((8 - npre, DN_QKV), F32)
        cbuf[8 - npre:8, :] = prev_ref[0]
        s_sc[...] = jnp.concatenate([s0_ref[0, h] for h in range(H)], axis=1)

    if nb > 1:
        @pl.when(t > 0)
        def _():
            cbuf[0:8, :] = cbuf[tb:tb + 8, :]

    cbuf[8:8 + tb, :] = qkv_ref[...]
    y = jnp.zeros((tb, DN_QKV), F32)
    for j in range(DN_CONV):
        y = y + cw_ref[j:j + 1, :] * cbuf[8 - npre + j:8 - npre + j + tb, :]
    y = _silu(y)

    @pl.when(t == nb - 1)
    def _():
        cnew_ref[0] = cbuf[8 + tb - npre:8 + tb, :]

    ba = ba_ref[...]
    beta = _sigmoid(ba)
    gg = -jnp.exp(av_ref[...]) * _softplus(ba + dv_ref[...])
    zz = z_ref[...]

    pad = C - tb if tb < C else 0
    rows = tb + pad

    def padr(a):
        if pad == 0:
            return a
        return jnp.concatenate([a, jnp.zeros((pad, a.shape[1]), a.dtype)], axis=0)

    def l2n(a, scale):
        parts = []
        for h in range(H):
            ah = a[:, h * LANES:(h + 1) * LANES]
            parts.append(ah * lax.rsqrt(jnp.sum(ah * ah, axis=-1, keepdims=True) + EPS) * scale)
        return jnp.concatenate(parts, axis=1)

    q_all = padr(l2n(y[:, 0:H * DN_DK], DN_DK ** -0.5))
    k_all = padr(l2n(y[:, H * DN_DK:2 * H * DN_DK], 1.0))
    v_all = padr(y[:, 2 * H * DN_DK:])
    beta = padr(beta)
    gg = padr(gg)

    ri = lax.broadcasted_iota(jnp.int32, (HC, HC), 0)
    ci = lax.broadcasted_iota(jnp.int32, (HC, HC), 1)
    same = (ri // C) == (ci // C)
    low = same & ((ri % C) >= (ci % C))
    strict = same & ((ri % C) > (ci % C))
    rowhead = lax.broadcasted_iota(jnp.int32, (HC, LANES), 0) // C
    tr = lax.broadcasted_iota(jnp.int32, (C, C), 0)
    tc = lax.broadcasted_iota(jnp.int32, (C, C), 1)
    tri = (tr >= tc).astype(F32)

    s_all = s_sc[...]
    outs = []
    for c in range(rows // C):
        r0 = c * C
        kst = _stack_heads(k_all[r0:r0 + C])
        qst = _stack_heads(q_all[r0:r0 + C])
        vst = _stack_heads(v_all[r0:r0 + C])
        gcm = _mm3(tri, gg[r0:r0 + C])
        bcol = _stack_cols(beta[r0:r0 + C], 0)
        gcol = _stack_cols(gcm, H)
        glast = jnp.concatenate(
            [jnp.broadcast_to(gcm[C - 1:C, H + h:H + h + 1], (C, 1)) for h in range(H)], axis=0)
        grow = jnp.transpose(jnp.broadcast_to(gcol, (HC, LANES)))[0:1, :]
        decay = jnp.where(low, jnp.exp(jnp.where(low, gcol - grow, 0.0)), 0.0)
        kb = kst * bcol
        kst16 = kst.astype(BF16)
        lmat = jnp.where(strict, _dot_nt(kb.astype(BF16), kst16) * decay, 0.0)
        attn = _dot_nt(qst.astype(BF16), kst16) * decay
        tinv = _unit_lower_inverse(lmat, ri, ci)
        egc = jnp.exp(gcol)
        sol = _mm3(tinv, jnp.concatenate([vst * bcol, kb * egc], axis=1))
        u = sol[:, :LANES]
        w = sol[:, LANES:]
        qd = qst * egc
        kd = kst * jnp.exp(glast - gcol)
        kdt = jnp.transpose(kd).astype(BF16)
        egl = jnp.concatenate(
            [jnp.broadcast_to(jnp.exp(gcm[C - 1:C, H + h:H + h + 1]), (1, LANES)) for h in range(H)],
            axis=1)
        r = _dot(jnp.concatenate([w, qd], axis=0).astype(BF16), s_all.astype(BF16))
        v_new = u - _diag_blocks(r[:HC], rowhead)
        o = _diag_blocks(r[HC:], rowhead) + _dot(attn.astype(BF16), v_new.astype(BF16))
        vbd = jnp.concatenate([jnp.where(rowhead == h, v_new, 0.0) for h in range(H)], axis=1)
        s_all = s_all * egl + _dot(kdt, vbd.astype(BF16))
        o = o * lax.rsqrt(jnp.mean(o * o, axis=-1, keepdims=True) + EPS) * ng_ref[...]
        outs.append(_unstack_heads(o, C))
    s_sc[...] = s_all
    o_all = outs[0] if len(outs) == 1 else jnp.concatenate(outs, axis=0)
    o_ref[...] = (o_all[:tb] * _silu(zz)).astype(o_ref.dtype)

    @pl.when(t == nb - 1)
    def _():
        for h in range(H):
            snew_ref[0, h] = s_all[:, h * DN_DV:(h + 1) * DN_DV]


def _deltanet(p, ba, prev, cw, avec, dvec, ng, s0, B, T, act_dtype):
    tb = min(T, 256)
    nb = T // tb
    return pl.pallas_call(
        functools.partial(_dn_kernel, tb=tb, nb=nb),
        grid=(B, nb),
        in_specs=[pl.BlockSpec((tb, DN_QKV), lambda b_, t: (b_ * nb + t, P_QKV // DN_QKV)),
                  pl.BlockSpec((tb, 512), lambda b_, t: (b_ * nb + t, P_Z // 512)),
                  pl.BlockSpec((tb, LANES), lambda b_, t: (b_ * nb + t, 0)),
                  pl.BlockSpec((1, DN_CONV - 1, DN_QKV), lambda b_, t: (b_, 0, 0)),
                  pl.BlockSpec((DN_CONV, DN_QKV), lambda b_, t: (0, 0)),
                  pl.BlockSpec((1, LANES), lambda b_, t: (0, 0)),
                  pl.BlockSpec((1, LANES), lambda b_, t: (0, 0)),
                  pl.BlockSpec((1, DN_DV), lambda b_, t: (0, 0)),
                  pl.BlockSpec((1, DN_HEADS, DN_DK, DN_DV), lambda b_, t: (b_, 0, 0, 0))],
        out_specs=[pl.BlockSpec((tb, DN_HEADS * DN_DV), lambda b_, t: (b_ * nb + t, 0)),
                   pl.BlockSpec((1, DN_CONV - 1, DN_QKV), lambda b_, t: (b_, 0, 0)),
                   pl.BlockSpec((1, DN_HEADS, DN_DK, DN_DV), lambda b_, t: (b_, 0, 0, 0))],
        out_shape=[SDS((B * T, DN_HEADS * DN_DV), act_dtype),
                   SDS((B, DN_CONV - 1, DN_QKV), F32),
                   SDS((B, DN_HEADS, DN_DK, DN_DV), F32)],
        scratch_shapes=[pltpu.VMEM((tb + 8, DN_QKV), F32),
                        pltpu.VMEM((DN_DK, DN_HEADS * DN_DV), F32)],
        compiler_params=_cparams("parallel", "arbitrary"),
        name="deltanet",
    )(p, p, ba, prev, cw, avec, dvec, ng, s0)


def _qknorm_kernel(q_ref, k_ref, v_ref, gq_ref, gk_ref, c_ref, s1_ref, s2_ref,
                   qo_ref, kf_ref, kb_ref, vb_ref):
    ri = lax.broadcasted_iota(jnp.int32, (LANES, LANES), 0)
    ci = lax.broadcasted_iota(jnp.int32, (LANES, LANES), 1)
    seg = ((ri // DIFF_DQK) == (ci // DIFF_DQK)).astype(BF16)
    cos = c_ref[...]
    s1 = s1_ref[...]
    s2 = s2_ref[...]

    def norm_rope(x, g):
        sq = x * x
        hi = sq.astype(BF16)
        r1 = sq - hi.astype(F32)
        mid = r1.astype(BF16)
        lo = (r1 - mid.astype(F32)).astype(BF16)
        tot = _dot(hi, seg) + (_dot(mid, seg) + _dot(lo, seg))
        y = x * lax.rsqrt(tot * (1.0 / DIFF_DQK) + EPS) * g
        return y * cos + pltpu.roll(y, LANES - ROT_DIM // 2, 1) * s1 + pltpu.roll(y, ROT_DIM // 2, 1) * s2

    for c in range(2 * DIFF_HEADS * DIFF_DQK // LANES):
        sl = slice(c * LANES, (c + 1) * LANES)
        qn = norm_rope(q_ref[:, sl], gq_ref[...])
        kn = norm_rope(k_ref[:, sl], gk_ref[...])
        qo_ref[:, sl] = (qn * (DIFF_DQK ** -0.5)).astype(qo_ref.dtype)
        kf_ref[:, sl] = kn
        kb_ref[:, sl] = kn.astype(kb_ref.dtype)
    vb_ref[...] = v_ref[...].astype(vb_ref.dtype)


def _qknorm(p, gq, gk, cos, s1, s2, n, T, act_dtype):
    tm = min(T, 512) if T % 16 == 0 else n
    nrep = T // tm if T % 16 == 0 else 1
    W = 1024
    tab = pl.BlockSpec((tm, LANES), lambda i: (i % nrep, 0))
    vec = pl.BlockSpec((1, LANES), lambda i: (0, 0))
    return pl.pallas_call(
        _qknorm_kernel,
        grid=(n // tm,),
        in_specs=[pl.BlockSpec((tm, W), lambda i: (i, P_DQ // W)),
                  pl.BlockSpec((tm, W), lambda i: (i, P_DK // W)),
                  pl.BlockSpec((tm, W), lambda i: (i, P_DV // W)),
                  vec, vec, tab, tab, tab],
        out_specs=[pl.BlockSpec((tm, W), lambda i: (i, 0))] * 4,
        out_shape=[SDS((n, W), act_dtype), SDS((n, W), F32), SDS((n, W), act_dtype), SDS((n, W), act_dtype)],
        compiler_params=_cparams("parallel"),
        name="qknorm_rope",
    )(p, p, p, gq, gk, cos, s1, s2)


def _rope_tables(pos):
    half = ROT_DIM // 2
    inv = ROPE_THETA ** (-jnp.arange(0, ROT_DIM, 2, dtype=F32) / ROT_DIM)
    ang = pos.astype(F32)[:, None] * inv[None, :]
    cos, sin = jnp.cos(ang), jnp.sin(ang)
    T = pos.shape[0]
    one = jnp.ones((T, DIFF_DQK - ROT_DIM), F32)
    zero = jnp.zeros((T, DIFF_DQK - ROT_DIM), F32)
    zh = jnp.zeros((T, half), F32)
    c64 = jnp.concatenate([cos, cos, one], axis=1)
    s1_64 = jnp.concatenate([-sin, zh, zero], axis=1)
    s2_64 = jnp.concatenate([zh, sin, zero], axis=1)
    return tuple(jnp.concatenate([a, a], axis=1) for a in (c64, s1_64, s2_64))


def _lambda(l1q, l1k, l2q, l2k, lam_init):
    a = jnp.sum(l1q * l1k, axis=-1, keepdims=True)
    b = jnp.sum(l2q * l2k, axis=-1, keepdims=True)
    return jnp.exp(a) - jnp.exp(b) + lam_init


def _diff_finish(acc0, l0, acc1, l1, lam, g, lam_init):
    o = acc0 / l0 - lam * (acc1 / l1)
    o = o * lax.rsqrt(jnp.mean(o * o, axis=-1, keepdims=True) + EPS) * g
    return o * (1.0 - lam_init)


def _flash_kernel(q_ref, k_ref, v_ref, l1q_ref, l1k_ref, l2q_ref, l2k_ref, g_ref, o_ref,
                  m_sc, l_sc, acc_sc, *, tq, lam_init):
    qi = pl.program_id(2)
    ki = pl.program_id(3)

    @pl.when(ki == 0)
    def _():
        m_sc[...] = jnp.full(m_sc.shape, NEG, F32)
        l_sc[...] = jnp.zeros(l_sc.shape, F32)
        acc_sc[...] = jnp.zeros(acc_sc.shape, F32)

    def step(masked):
        q = q_ref[...]
        k = k_ref[...]
        v = v_ref[...]
        lane = lax.broadcasted_iota(jnp.int32, (1, LANES), 1)
        if masked:
            rr = lax.broadcasted_iota(jnp.int32, (tq, tq), 0)
            cc = lax.broadcasted_iota(jnp.int32, (tq, tq), 1)
            keep = cc <= rr
        for j in range(2):
            qj = jnp.where((lane // DIFF_DQK) == j, q, jnp.zeros_like(q))
            s = _dot_nt(qj, k)
            if masked:
                s = jnp.where(keep, s, NEG)
            m_prev = m_sc[j]
            m_new = jnp.maximum(m_prev, jnp.max(s, axis=-1, keepdims=True))
            alpha = jnp.exp(m_prev - m_new)
            p = jnp.exp(s - m_new)
            l_sc[j] = alpha * l_sc[j] + jnp.sum(p, axis=-1, keepdims=True)
            acc_sc[j] = alpha * acc_sc[j] + _dot(p.astype(BF16), v)
            m_sc[j] = m_new

    @pl.when(ki < qi)
    def _():
        step(False)

    @pl.when(ki == qi)
    def _():
        step(True)
        lam = _lambda(l1q_ref[...], l1k_ref[...], l2q_ref[...], l2k_ref[...], lam_init)
        o = _diff_finish(acc_sc[0], l_sc[0], acc_sc[1], l_sc[1], lam, g_ref[...], lam_init)
        o_ref[...] = o.astype(o_ref.dtype)


def _flash_attention(q, k, v, lparams, g, B, T, lam_init):
    tq = min(T, 512)
    nq = T // tq
    vec64 = pl.BlockSpec((1, DIFF_DQK), lambda b_, h, qi, ki: (0, 0))
    return pl.pallas_call(
        functools.partial(_flash_kernel, tq=tq, lam_init=lam_init),
        grid=(B, DIFF_HEADS, nq, nq),
        in_specs=[pl.BlockSpec((tq, LANES), lambda b_, h, qi, ki: (b_ * nq + qi, h)),
                  pl.BlockSpec((tq, LANES), lambda b_, h, qi, ki: (b_ * nq + jnp.minimum(ki, qi), h)),
                  pl.BlockSpec((tq, LANES), lambda b_, h, qi, ki: (b_ * nq + jnp.minimum(ki, qi), h)),
                  vec64, vec64, vec64, vec64,
                  pl.BlockSpec((1, DIFF_DV), lambda b_, h, qi, ki: (0, 0))],
        out_specs=pl.BlockSpec((tq, LANES), lambda b_, h, qi, ki: (b_ * nq + qi, h)),
        out_shape=SDS((B * T, DIFF_HEADS * DIFF_DV), BF16),
        scratch_shapes=[pltpu.VMEM((2, tq, 1), F32), pltpu.VMEM((2, tq, 1), F32),
                        pltpu.VMEM((2, tq, DIFF_DV), F32)],
        compiler_params=_cparams("parallel", "parallel", "parallel", "arbitrary"),
        name="flash_diff_attention",
    )(q, k, v, *lparams, g)


def _paged_kernel(pt_ref, q_ref, kn_ref, vn_ref, l1q_ref, l1k_ref, l2q_ref, l2k_ref, g_ref, *rest,
                  pps, nsteps, tnew, lam_init):
    del pt_ref
    kv_refs = rest[:2 * pps]
    o_ref, m_sc, l_sc, acc_sc = rest[2 * pps:]
    s_id = pl.program_id(1)
    hv_n = DIFF_HEADS
    nrow = hv_n * tnew

    @pl.when(s_id == 0)
    def _():
        m_sc[...] = jnp.full(m_sc.shape, NEG, F32)
        l_sc[...] = jnp.zeros(l_sc.shape, F32)
        acc_sc[...] = jnp.zeros(acc_sc.shape, F32)

    def update(j, kj, vflat, keep):
        s = _dot_nt(q_ref[0, j], kj)
        s = jnp.where(keep, s, NEG)
        m_prev = m_sc[j]
        m_new = jnp.maximum(m_prev, jnp.max(s, axis=-1, keepdims=True))
        alpha = jnp.exp(m_prev - m_new)
        p = jnp.where(keep, jnp.exp(s - m_new), 0.0)
        l_sc[j] = alpha * l_sc[j] + jnp.sum(p, axis=-1, keepdims=True)
        acc_sc[j] = alpha * acc_sc[j] + _dot(p.astype(BF16), vflat)
        m_sc[j] = m_new

    @pl.when(s_id < nsteps - 1)
    def _():
        page = kv_refs[0].shape[0]
        ncol = page * hv_n
        rr = lax.broadcasted_iota(jnp.int32, (nrow, ncol), 0)
        cc = lax.broadcasted_iota(jnp.int32, (nrow, ncol), 1)
        keep = (cc % hv_n) == (rr // tnew)
        for r in range(pps):
            k_ref = kv_refs[r]
            v_ref = kv_refs[pps + r]
            vflat = v_ref[...].reshape(ncol, DIFF_DV).astype(BF16)
            for j in range(2):
                kj = k_ref[:, pl.ds(j, hv_n, stride=2), :].reshape(ncol, DIFF_DQK).astype(BF16)
                update(j, kj, vflat, keep)

    @pl.when(s_id == nsteps - 1)
    def _():
        ncol = tnew * hv_n
        rr = lax.broadcasted_iota(jnp.int32, (nrow, ncol), 0)
        cc = lax.broadcasted_iota(jnp.int32, (nrow, ncol), 1)
        keep = ((cc % hv_n) == (rr // tnew)) & ((cc // hv_n) <= (rr % tnew))
        vflat = vn_ref[0].reshape(ncol, DIFF_DV).astype(BF16)
        for j in range(2):
            kj = kn_ref[0, :, pl.ds(j, hv_n, stride=2), :].reshape(ncol, DIFF_DQK).astype(BF16)
            update(j, kj, vflat, keep)
        lam = _lambda(l1q_ref[...], l1k_ref[...], l2q_ref[...], l2k_ref[...], lam_init)
        o_ref[0] = _diff_finish(acc_sc[0], l_sc[0], acc_sc[1], l_sc[1], lam, g_ref[...], lam_init)


def _paged_attention(q_arr, k_new, v_new, cache_k, cache_v, page_table, layer, lparams, g, lam_init):
    DB, n_pages = page_table.shape
    tnew = k_new.shape[1]
    page = cache_k.shape[2]
    pps = 4 if n_pages % 4 == 0 else 1
    nsteps = n_pages // pps + 1
    nrow = DIFF_HEADS * tnew

    def kv_spec(r, shape):
        def imap(b_, s, pt):
            idx = jnp.minimum(s * pps + r, n_pages - 1)
            return (layer, pt[b_ * n_pages + idx], 0, 0, 0)
        return pl.BlockSpec((None, None) + shape, imap)

    vec64 = pl.BlockSpec((1, DIFF_DQK), lambda b_, s, pt: (0, 0))
    in_specs = [pl.BlockSpec((1, 2, nrow, DIFF_DQK), lambda b_, s, pt: (b_, 0, 0, 0)),
                pl.BlockSpec((1, tnew, 2 * DIFF_HEADS, DIFF_DQK), lambda b_, s, pt: (b_, 0, 0, 0)),
                pl.BlockSpec((1, tnew, DIFF_HEADS, DIFF_DV), lambda b_, s, pt: (b_, 0, 0, 0)),
                vec64, vec64, vec64, vec64,
                pl.BlockSpec((1, DIFF_DV), lambda b_, s, pt: (0, 0))]
    in_specs += [kv_spec(r, (page, 2 * DIFF_HEADS, DIFF_DQK)) for r in range(pps)]
    in_specs += [kv_spec(r, (page, DIFF_HEADS, DIFF_DV)) for r in range(pps)]
    grid_spec = pltpu.PrefetchScalarGridSpec(
        num_scalar_prefetch=1,
        grid=(DB, nsteps),
        in_specs=in_specs,
        out_specs=pl.BlockSpec((1, nrow, DIFF_DV), lambda b_, s, pt: (b_, 0, 0)),
        scratch_shapes=[pltpu.VMEM((2, nrow, 1), F32), pltpu.VMEM((2, nrow, 1), F32),
                        pltpu.VMEM((2, nrow, DIFF_DV), F32)])
    return pl.pallas_call(
        functools.partial(_paged_kernel, pps=pps, nsteps=nsteps, tnew=tnew, lam_init=lam_init),
        grid_spec=grid_spec,
        out_shape=SDS((DB, nrow, DIFF_DV), F32),
        compiler_params=_cparams("parallel", "arbitrary"),
        name="paged_diff_attention",
    )(page_table.reshape(-1), q_arr, k_new, v_new, *lparams, g,
      *([cache_k] * pps), *([cache_v] * pps))


def _outproj_kernel(a_ref, b_ref, c_ref, wa_ref, wb_ref, wc_ref, x_ref, o_ref):
    acc = _dot(a_ref[...].astype(BF16), wa_ref[...])
    acc = acc + _dot(b_ref[...].astype(BF16), wb_ref[...])
    acc = acc + _dot(c_ref[...].astype(BF16), wc_ref[...])
    o_ref[...] = x_ref[...] + acc


def _out_proj(conf, odn, att, w_out, x):
    n = x.shape[0]
    tm = min(n, 1024)
    tn = 512
    return pl.pallas_call(
        _outproj_kernel,
        grid=(n // tm, D_MODEL // tn),
        in_specs=[pl.BlockSpec((tm, 512), lambda i, j: (i, 0)),
                  pl.BlockSpec((tm, 512), lambda i, j: (i, 0)),
                  pl.BlockSpec((tm, 1024), lambda i, j: (i, 0)),
                  pl.BlockSpec((512, tn), lambda i, j: (0, j)),
                  pl.BlockSpec((512, tn), lambda i, j: (1, j)),
                  pl.BlockSpec((1024, tn), lambda i, j: (1, j)),
                  pl.BlockSpec((tm, tn), lambda i, j: (i, j))],
        out_specs=pl.BlockSpec((tm, tn), lambda i, j: (i, j)),
        out_shape=SDS((n, D_MODEL), F32),
        compiler_params=_cparams("parallel", "arbitrary"),
        name="out_proj",
    )(conf, odn, att, w_out, w_out, w_out, x)


def _router_kernel(x_ref, g_ref, wr_ref, br_ref, h_ref, gate_ref):
    x = x_ref[...]
    y = x * lax.rsqrt(jnp.mean(x * x, axis=-1, keepdims=True) + EPS)
    h = y * g_ref[...]
    h_ref[...] = h.astype(h_ref.dtype)
    logits = _mm3(h, wr_ref[...]) + br_ref[...]
    lane = lax.broadcasted_iota(jnp.int32, logits.shape, 1)
    big = jnp.int32(1 << 20)
    is_g = (lane >= N_EXPERTS) & (lane < N_EXPERTS + N_GROUPS)
    gl = jnp.where(is_g, logits, NEG)
    gmax = jnp.max(gl, axis=-1, keepdims=True)
    gsel = jnp.min(jnp.where(is_g & (gl == gmax), lane, big), axis=-1, keepdims=True) - N_EXPERTS
    gw = 1.0 / jnp.sum(jnp.where(is_g, jnp.exp(gl - gmax), 0.0), axis=-1, keepdims=True)
    in_grp = (lane >= gsel * EXPERTS_PER_GROUP) & (lane < (gsel + 1) * EXPERTS_PER_GROUP)
    el = jnp.where(in_grp, logits, NEG)
    v1 = jnp.max(el, axis=-1, keepdims=True)
    i1 = jnp.min(jnp.where(in_grp & (el == v1), lane, big), axis=-1, keepdims=True)
    rest = in_grp & (lane != i1)
    el2 = jnp.where(rest, logits, NEG)
    v2 = jnp.max(el2, axis=-1, keepdims=True)
    i2 = jnp.min(jnp.where(rest & (el2 == v2), lane, big), axis=-1, keepdims=True)
    e2 = jnp.exp(v2 - v1)
    den = 1.0 + e2
    gate_ref[...] = jnp.where(lane == i1, (1.0 / den) * gw, 0.0) + jnp.where(lane == i2, (e2 / den) * gw, 0.0)


def _router(x, g, wr, br, act_dtype):
    n = x.shape[0]
    tm = min(n, 512)
    return pl.pallas_call(
        _router_kernel,
        grid=(n // tm,),
        in_specs=[pl.BlockSpec((tm, D_MODEL), lambda i: (i, 0)),
                  pl.BlockSpec((1, D_MODEL), lambda i: (0, 0)),
                  pl.BlockSpec((D_MODEL, LANES), lambda i: (0, 0)),
                  pl.BlockSpec((1, LANES), lambda i: (0, 0))],
        out_specs=[pl.BlockSpec((tm, D_MODEL), lambda i: (i, 0)),
                   pl.BlockSpec((tm, LANES), lambda i: (i, 0))],
        out_shape=[SDS((n, D_MODEL), act_dtype), SDS((n, LANES), F32)],
        compiler_params=_cparams("parallel"),
        name="norm_router",
    )(x, g, wr, br)


def _moe_kernel(h_ref, gate_ref, x_ref, wg_ref, wu_ref, wd_ref, o_ref):
    e = pl.program_id(1)

    @pl.when(e == 0)
    def _():
        o_ref[...] = x_ref[...]

    h = h_ref[...].astype(BF16)
    hg = _dot(h, wg_ref[...])
    hu = _dot(h, wu_ref[...])
    lane = lax.broadcasted_iota(jnp.int32, gate_ref.shape, 1)
    ge = jnp.sum(jnp.where(lane == e, gate_ref[...], 0.0), axis=-1, keepdims=True)
    act = _silu(hg) * hu * ge
    o_ref[...] += _dot(act.astype(BF16), wd_ref[...])


def _moe(h, gates, x, wg, wu, wd):
    n = x.shape[0]
    tm = min(n, 512)
    return pl.pallas_call(
        _moe_kernel,
        grid=(n // tm, N_EXPERTS),
        in_specs=[pl.BlockSpec((tm, D_MODEL), lambda i, e: (i, 0)),
                  pl.BlockSpec((tm, LANES), lambda i, e: (i, 0)),
                  pl.BlockSpec((tm, D_MODEL), lambda i, e: (i, 0)),
                  pl.BlockSpec((None, D_MODEL, D_EXPERT), lambda i, e: (e, 0, 0)),
                  pl.BlockSpec((None, D_MODEL, D_EXPERT), lambda i, e: (e, 0, 0)),
                  pl.BlockSpec((None, D_EXPERT, D_MODEL), lambda i, e: (e, 0, 0))],
        out_specs=pl.BlockSpec((tm, D_MODEL), lambda i, e: (i, 0)),
        out_shape=SDS((n, D_MODEL), F32),
        compiler_params=_cparams("parallel", "arbitrary"),
        name="moe_experts",
    )(h, gates, x, wg, wu, wd)


def _row(v, width=None):
    v = v.astype(F32).reshape(1, -1)
    if width is not None and v.shape[1] < width:
        v = jnp.pad(v, ((0, 0), (0, width - v.shape[1])))
    return v


def _prep_layer(l, w):
    wi = w['w_in'][l]
    w_main = jnp.concatenate([wi[:, 1024:2560], wi[:, 2560:3072], wi[:, 0:1024], wi[:, 3080:6152]],
                             axis=1).astype(BF16)
    w_ba = jnp.pad(wi[:, 3072:3080], ((0, 0), (0, LANES - 8))).astype(BF16)
    wr = jnp.concatenate([w['router_expert_w'][l], w['router_group_w'][l]], axis=1)
    wr = jnp.pad(wr, ((0, 0), (0, LANES - wr.shape[1])))
    br = _row(jnp.concatenate([w['router_expert_b'][l], w['router_group_b'][l]]), LANES)
    lane_pad = lambda v: _row(jnp.concatenate([jnp.zeros((DN_HEADS,), F32), v.astype(F32)]), LANES)
    return dict(
        norm1_g=_row(w['norm1_g'][l]), w_main=w_main, w_ba=w_ba,
        conf_w=w['conf_dw_w'][l], conf_b=_row(w['conf_dw_b'][l]),
        conf_g=_row(w['conf_ln_g'][l]), conf_bb=_row(w['conf_ln_b'][l]),
        dn_cw=w['dn_conv_w'][l], dn_av=lane_pad(w['dn_a_log'][l]), dn_dv=lane_pad(w['dn_dt_bias'][l]),
        dn_ng=_row(w['dn_norm_g'][l]),
        gq=_row(jnp.tile(w['diff_q_norm_g'][l], 2)), gk=_row(jnp.tile(w['diff_k_norm_g'][l], 2)),
        lparams=tuple(_row(w[k][l]) for k in ('diff_lambda_q1', 'diff_lambda_k1',
                                               'diff_lambda_q2', 'diff_lambda_k2')),
        subln=_row(w['diff_subln_g'][l]),
        w_out=w['w_out'][l].astype(BF16), norm2_g=_row(w['norm2_g'][l]), wr=wr, br=br,
        wg=w['moe_w_gate'][l].astype(BF16), wu=w['moe_w_up'][l].astype(BF16),
        wd=w['moe_w_down'][l].astype(BF16))


def _layer(x, B, T, tables, conf_prev, dnc_prev, dn_s0, lp, lam_init, paged):
    n = B * T
    act_dtype = BF16 if T % 16 == 0 else F32
    p, ba = _in_proj(x, lp['norm1_g'], lp['w_main'], lp['w_ba'])
    conf, conf_new = _conformer(p, conf_prev, lp['conf_w'], lp['conf_b'], lp['conf_g'], lp['conf_bb'],
                                B, T, act_dtype)
    odn, dnc_new, dn_state = _deltanet(p, ba, dnc_prev, lp['dn_cw'], lp['dn_av'], lp['dn_dv'],
                                       lp['dn_ng'], dn_s0, B, T, act_dtype)
    qn, kf, kb, vb = _qknorm(p, lp['gq'], lp['gk'], *tables, n, T, act_dtype)
    v_f32 = p[:, P_DV:P_DV + DIFF_HEADS * DIFF_DV]
    k_out = kf.reshape(B, T, 2 * DIFF_HEADS, DIFF_DQK)
    v_out = v_f32.reshape(B, T, DIFF_HEADS, DIFF_DV)
    if paged is None:
        att = _flash_attention(qn, kb, vb, lp['lparams'], lp['subln'], B, T, lam_init)
    else:
        cache_k, cache_v, page_table, layer = paged
        q_arr = qn.reshape(B, T, DIFF_HEADS, 2, DIFF_DQK).transpose(0, 3, 2, 1, 4)
        q_arr = q_arr.reshape(B, 2, DIFF_HEADS * T, DIFF_DQK).astype(BF16)
        att = _paged_attention(q_arr, k_out, v_out, cache_k, cache_v, page_table, layer,
                               lp['lparams'], lp['subln'], lam_init)
        att = att.reshape(B, DIFF_HEADS, T, DIFF_DV).transpose(0, 2, 1, 3).reshape(n, DIFF_HEADS * DIFF_DV)
    x2 = _out_proj(conf, odn, att, lp['w_out'], x)
    h2, gates = _router(x2, lp['norm2_g'], lp['wr'], lp['br'], act_dtype)
    x3 = _moe(h2, gates, x2, lp['wg'], lp['wu'], lp['wd'])
    return x3, (k_out, v_out, dn_state, dnc_new, conf_new)


def kernel(x_prompt, x_sample, cache_k, cache_v, page_table, state_delta, state_delta_conv, state_conf_conv, norm1_g, w_in, conf_dw_w, conf_dw_b, conf_ln_g, conf_ln_b, dn_conv_w, dn_a_log, dn_dt_bias, dn_norm_g, diff_q_norm_g, diff_k_norm_g, diff_lambda_q1, diff_lambda_k1, diff_lambda_q2, diff_lambda_k2, diff_subln_g, w_out, norm2_g, router_group_w, router_group_b, router_expert_w, router_expert_b, moe_w_gate, moe_w_up, moe_w_down):
    w = dict(norm1_g=norm1_g, w_in=w_in, conf_dw_w=conf_dw_w, conf_dw_b=conf_dw_b, conf_ln_g=conf_ln_g,
             conf_ln_b=conf_ln_b, dn_conv_w=dn_conv_w, dn_a_log=dn_a_log, dn_dt_bias=dn_dt_bias,
             dn_norm_g=dn_norm_g, diff_q_norm_g=diff_q_norm_g, diff_k_norm_g=diff_k_norm_g,
             diff_lambda_q1=diff_lambda_q1, diff_lambda_k1=diff_lambda_k1, diff_lambda_q2=diff_lambda_q2,
             diff_lambda_k2=diff_lambda_k2, diff_subln_g=diff_subln_g, w_out=w_out, norm2_g=norm2_g,
             router_group_w=router_group_w, router_group_b=router_group_b,
             router_expert_w=router_expert_w, router_expert_b=router_expert_b,
             moe_w_gate=moe_w_gate, moe_w_up=moe_w_up, moe_w_down=moe_w_down)
    B, S, _ = x_prompt.shape
    DB, T, _ = x_sample.shape
    depth = w_in.shape[0]
    past_len = page_table.shape[1] * cache_k.shape[2]
    tab_p = _rope_tables(jnp.arange(S, dtype=jnp.int32))
    tab_s = tuple(jnp.tile(a, (DB, 1)) for a in _rope_tables(past_len + jnp.arange(T, dtype=jnp.int32)))
    hp = x_prompt.reshape(B * S, D_MODEL)
    hs = x_sample.reshape(DB * T, D_MODEL)
    zero_conf = jnp.zeros((B, CONF_K - 1, CONF_W), F32)
    zero_dnc = jnp.zeros((B, DN_CONV - 1, DN_QKV), F32)
    zero_s = jnp.zeros((B, DN_HEADS, DN_DK, DN_DV), F32)
    outs_p, outs_s = [], []
    for l in range(depth):
        lp = _prep_layer(l, w)
        lam_init = 0.8 - 0.6 * math.exp(-0.3 * l)
        hp, sp = _layer(hp, B, S, tab_p, zero_conf, zero_dnc, zero_s, lp, lam_init, None)
        hs, ss = _layer(hs, DB, T, tab_s, state_conf_conv[l], state_delta_conv[l], state_delta[l], lp,
                        lam_init, (cache_k, cache_v, page_table, l))
        outs_p.append(sp)
        outs_s.append(ss)
    st = lambda outs, i: jnp.stack([o[i] for o in outs])
    return (hp.reshape(B, S, D_MODEL), hs.reshape(DB, T, D_MODEL),
            st(outs_p, 0), st(outs_p, 1), st(outs_s, 0), st(outs_s, 1),
            st(outs_p, 2), st(outs_s, 2), st(outs_p, 3), st(outs_s, 3),
            st(outs_p, 4), st(outs_s, 4))
```

```python
import functools
import math

import jax
import jax.numpy as jnp
from jax import lax
from jax.experimental import pallas as pl
from jax.experimental.pallas import tpu as pltpu

F32 = jnp.float32
BF16 = jnp.bfloat16
SDS = jax.ShapeDtypeStruct

D_MODEL = 2048
CONF_W = 512
CONF_K = 31
DN_HEADS = 4
DN_DK = 128
DN_DV = 128
DN_CONV = 4
DN_CHUNK = 64
DN_QKV = DN_HEADS * (2 * DN_DK + DN_DV)
DIFF_HEADS = 8
DIFF_DQK = 64
DIFF_DV = 128
ROT_DIM = 16
ROPE_THETA = 500000.0
N_GROUPS = 4
EXPERTS_PER_GROUP = 4
N_EXPERTS = 16
D_EXPERT = 512
EPS = 1e-6
LANES = 128
VMEM_LIMIT = 48 * 1024 * 1024

P_QKV, P_Z, P_GLU, P_DQ, P_DK, P_DV = 0, 1536, 2048, 3072, 4096, 5120
P_W = 6144
NEG = -1e30


def _cparams(*sem):
    return pltpu.CompilerParams(dimension_semantics=sem, vmem_limit_bytes=VMEM_LIMIT)


def _dot(a, b):
    return jnp.dot(a, b, preferred_element_type=F32)


def _dot_nt(a, b):
    return lax.dot_general(a, b, (((1,), (1,)), ((), ())), preferred_element_type=F32)


def _split(a):
    hi = a.astype(BF16)
    return hi, (a - hi.astype(F32)).astype(BF16)


def _mm3(a, b):
    ah, al = _split(a)
    bh, bl = _split(b)
    return _dot(ah, bh) + (_dot(ah, bl) + _dot(al, bh))


def _sigmoid(x):
    return 1.0 / (1.0 + jnp.exp(-x))


def _silu(x):
    return x * _sigmoid(x)


def _softplus(x):
    return jnp.maximum(x, 0.0) + jnp.log1p(jnp.exp(-jnp.abs(x)))


def _inproj_kernel(x_ref, g_ref, w_ref, wba_ref, o_ref, oba_ref, h_ref):
    @pl.when(pl.program_id(1) == 0)
    def _():
        x = x_ref[...]
        y = x * lax.rsqrt(jnp.mean(x * x, axis=-1, keepdims=True) + EPS)
        h_ref[...] = (y * g_ref[...]).astype(BF16)
        oba_ref[...] = _dot(h_ref[...], wba_ref[...])

    o_ref[...] = _dot(h_ref[...], w_ref[...])


def _in_proj(x, g, w_main, w_ba):
    n = x.shape[0]
    tm = min(n, 1024)
    tn = 512
    return pl.pallas_call(
        _inproj_kernel,
        grid=(n // tm, P_W // tn),
        in_specs=[pl.BlockSpec((tm, D_MODEL), lambda i, j: (i, 0)),
                  pl.BlockSpec((1, D_MODEL), lambda i, j: (0, 0)),
                  pl.BlockSpec((D_MODEL, tn), lambda i, j: (0, j)),
                  pl.BlockSpec((D_MODEL, LANES), lambda i, j: (0, 0))],
        out_specs=[pl.BlockSpec((tm, tn), lambda i, j: (i, j)),
                   pl.BlockSpec((tm, LANES), lambda i, j: (i, 0))],
        out_shape=[SDS((n, P_W), F32), SDS((n, LANES), F32)],
        scratch_shapes=[pltpu.VMEM((tm, D_MODEL), BF16)],
        compiler_params=_cparams("parallel", "arbitrary"),
        name="in_proj",
    )(x, g, w_main, w_ba)


def _conf_kernel(p_ref, prev_ref, w_ref, b_ref, g_ref, bb_ref, o_ref, new_ref, ubuf, *, tt, nt):
    t = pl.program_id(1)
    hist = 32

    @pl.when(t == 0)
    def _():
        ubuf[0:2, :] = jnp.zeros((2, CONF_W), F32)
        ubuf[2:hist, :] = prev_ref[0]

    if nt > 1:
        @pl.when(t > 0)
        def _():
            ubuf[0:hist, :] = ubuf[tt:tt + hist, :]

    x = p_ref[...]
    ubuf[hist:hist + tt, :] = x[:, :CONF_W] * _sigmoid(x[:, CONF_W:])
    rc = min(tt, 64)
    for r0 in range(0, tt, rc):
        acc = jnp.zeros((rc, CONF_W), F32)
        for j in range(CONF_K):
            s = r0 + hist - (CONF_K - 1) + j
            acc = acc + w_ref[j:j + 1, :] * ubuf[s:s + rc, :]
        c = acc + b_ref[...]
        xc = c - jnp.mean(c, axis=-1, keepdims=True)
        y = xc * lax.rsqrt(jnp.mean(xc * xc, axis=-1, keepdims=True) + EPS)
        y = y * g_ref[...] + bb_ref[...]
        o_ref[r0:r0 + rc, :] = _silu(y).astype(o_ref.dtype)

    @pl.when(t == nt - 1)
    def _():
        new_ref[0] = ubuf[tt + hist - (CONF_K - 1):tt + hist, :]


def _conformer(p, prev, w, b, g, bb, B, T, act_dtype):
    tt = min(T, 256)
    nt = T // tt
    return pl.pallas_call(
        functools.partial(_conf_kernel, tt=tt, nt=nt),
        grid=(B, nt),
        in_specs=[pl.BlockSpec((tt, 2 * CONF_W), lambda b_, t: (b_ * nt + t, P_GLU // (2 * CONF_W))),
                  pl.BlockSpec((1, CONF_K - 1, CONF_W), lambda b_, t: (b_, 0, 0)),
                  pl.BlockSpec((CONF_K, CONF_W), lambda b_, t: (0, 0)),
                  pl.BlockSpec((1, CONF_W), lambda b_, t: (0, 0)),
                  pl.BlockSpec((1, CONF_W), lambda b_, t: (0, 0)),
                  pl.BlockSpec((1, CONF_W), lambda b_, t: (0, 0))],
        out_specs=[pl.BlockSpec((tt, CONF_W), lambda b_, t: (b_ * nt + t, 0)),
                   pl.BlockSpec((1, CONF_K - 1, CONF_W), lambda b_, t: (b_, 0, 0))],
        out_shape=[SDS((B * T, CONF_W), act_dtype), SDS((B, CONF_K - 1, CONF_W), F32)],
        scratch_shapes=[pltpu.VMEM((tt + 32, CONF_W), F32)],
        compiler_params=_cparams("parallel", "arbitrary"),
        name="conformer",
    )(p, prev, w, b, g, bb)


def _stack_heads(a):
    return jnp.concatenate([a[:, h * LANES:(h + 1) * LANES] for h in range(DN_HEADS)], axis=0)


def _unstack_heads(a, c):
    return jnp.concatenate([a[h * c:(h + 1) * c, :] for h in range(DN_HEADS)], axis=1)


def _stack_cols(a, lane0):
    return jnp.concatenate([a[:, lane0 + h:lane0 + h + 1] for h in range(DN_HEADS)], axis=0)


def _diag_blocks(x, rowhead):
    out = jnp.where(rowhead == 0, x[:, 0:LANES], 0.0)
    for h in range(1, DN_HEADS):
        out = out + jnp.where(rowhead == h, x[:, h * LANES:(h + 1) * LANES], 0.0)
    return out


def _unit_lower_inverse(lmat, ri, ci):
    n = lmat.shape[0]
    eye = (ri == ci).astype(F32)
    d = jnp.where((ri // 16) == (ci // 16), lmat, 0.0)
    d2 = _mm3(d, d)
    d4 = _mm3(d2, d2)
    d8 = _mm3(d4, d4)
    t = _mm3(_mm3(_mm3(eye - d, eye + d2), eye + d4), eye + d8)
    m32 = jnp.where(((ri // 32) == (ci // 32)) & ((ri // 16) != (ci // 16)), lmat, 0.0)
    t = t - _mm3(_mm3(t, m32), t)
    m64 = jnp.where((ri // 32) != (ci // 32), lmat, 0.0)
    t = t - _mm3(_mm3(t, m64), t)
    del n
    return t


def _dn_kernel(qkv_ref, z_ref, ba_ref, prev_ref, cw_ref, av_ref, dv_ref, ng_ref, s0_ref,
               o_ref, cnew_ref, snew_ref, cbuf, s_sc, *, tb, nb):
    t = pl.program_id(1)
    C = DN_CHUNK
    H = DN_HEADS
    HC = H * C
    npre = DN_CONV - 1

    @pl.when(t == 0)
    def _():
        cbuf[0:8 - npre, :] = jnp.zeros((8 - npre, DN_QKV), F32)
        cbuf[8 - npre:8, :] = prev_ref[0]
        s_sc[...] = jnp.concatenate([s0_ref[0, h] for h in range(H)], axis=1)

    if nb > 1:
        @pl.when(t > 0)
        def _():
            cbuf[0:8, :] = cbuf[tb:tb + 8, :]

    cbuf[8:8 + tb, :] = qkv_ref[...]
    y = jnp.zeros((tb, DN_QKV), F32)
    for j in range(DN_CONV):
        y = y + cw_ref[j:j + 1, :] * cbuf[8 - npre + j:8 - npre + j + tb, :]
    y = _silu(y)

    @pl.when(t == nb - 1)
    def _():
        cnew_ref[0] = cbuf[8 + tb - npre:8 + tb, :]

    ba = ba_ref[...]
    beta = _sigmoid(ba)
    gg = -jnp.exp(av_ref[...]) * _softplus(ba + dv_ref[...])
    zz = z_ref[...]

    pad = C - tb if tb < C else 0
    rows = tb + pad

    def padr(a):
        if pad == 0:
            return a
        return jnp.concatenate([a, jnp.zeros((pad, a.shape[1]), a.dtype)], axis=0)

    def l2n(a, scale):
        parts = []
        for h in range(H):
            ah = a[:, h * LANES:(h + 1) * LANES]
            parts.append(ah * lax.rsqrt(jnp.sum(ah * ah, axis=-1, keepdims=True) + EPS) * scale)
        return jnp.concatenate(parts, axis=1)

    q_all = padr(l2n(y[:, 0:H * DN_DK], DN_DK ** -0.5))
    k_all = padr(l2n(y[:, H * DN_DK:2 * H * DN_DK], 1.0))
    v_all = padr(y[:, 2 * H * DN_DK:])
    beta = padr(beta)
    gg = padr(gg)

    ri = lax.broadcasted_iota(jnp.int32, (HC, HC), 0)
    ci = lax.broadcasted_iota(jnp.int32, (HC, HC), 1)
    same = (ri // C) == (ci // C)
    low = same & ((ri % C) >= (ci % C))
    strict = same & ((ri % C) > (ci % C))
    rowhead = lax.broadcasted_iota(jnp.int32, (HC, LANES), 0) // C
    tr = lax.broadcasted_iota(jnp.int32, (C, C), 0)
    tc = lax.broadcasted_iota(jnp.int32, (C, C), 1)
    tri = (tr >= tc).astype(F32)

    s_all = s_sc[...]
    outs = []
    for c in range(rows // C):
        r0 = c * C
        kst = _stack_heads(k_all[r0:r0 + C])
        qst = _stack_heads(q_all[r0:r0 + C])
        vst = _stack_heads(v_all[r0:r0 + C])
        gcm = _mm3(tri, gg[r0:r0 + C])
        bcol = _stack_cols(beta[r0:r0 + C], 0)
        gcol = _stack_cols(gcm, H)
        glast = jnp.concatenate(
            [jnp.broadcast_to(gcm[C - 1:C, H + h:H + h + 1], (C, 1)) for h in range(H)], axis=0)
        grow = jnp.transpose(jnp.broadcast_to(gcol, (HC, LANES)))[0:1, :]
        decay = jnp.where(low, jnp.exp(jnp.where(low, gcol - grow, 0.0)), 0.0)
        kb = kst * bcol
        kst16 = kst.astype(BF16)
        lmat = jnp.where(strict, _dot_nt(kb.astype(BF16), kst16) * decay, 0.0)
        attn = _dot_nt(qst.astype(BF16), kst16) * decay
        tinv = _unit_lower_inverse(lmat, ri, ci)
        egc = jnp.exp(gcol)
        sol = _mm3(tinv, jnp.concatenate([vst * bcol, kb * egc], axis=1))
        u = sol[:, :LANES]
        w = sol[:, LANES:]
        qd = qst * egc
        kd = kst * jnp.exp(glast - gcol)
        kdt = jnp.transpose(kd).astype(BF16)
        egl = jnp.concatenate(
            [jnp.broadcast_to(jnp.exp(gcm[C - 1:C, H + h:H + h + 1]), (1, LANES)) for h in range(H)],
            axis=1)
        r = _dot(jnp.concatenate([w, qd], axis=0).astype(BF16), s_all.astype(BF16))
        v_new = u - _diag_blocks(r[:HC], rowhead)
        o = _diag_blocks(r[HC:], rowhead) + _dot(attn.astype(BF16), v_new.astype(BF16))
        vbd = jnp.concatenate([jnp.where(rowhead == h, v_new, 0.0) for h in range(H)], axis=1)
        s_all = s_all * egl + _dot(kdt, vbd.astype(BF16))
        o = o * lax.rsqrt(jnp.mean(o * o, axis=-1, keepdims=True) + EPS) * ng_ref[...]
        outs.append(_unstack_heads(o, C))
    s_sc[...] = s_all
    o_all = outs[0] if len(outs) == 1 else jnp.concatenate(outs, axis=0)
    o_ref[...] = (o_all[:tb] * _silu(zz)).astype(o_ref.dtype)

    @pl.when(t == nb - 1)
    def _():
        for h in range(H):
            snew_ref[0, h] = s_all[:, h * DN_DV:(h + 1) * DN_DV]


def _deltanet(p, ba, prev, cw, avec, dvec, ng, s0, B, T, act_dtype):
    tb = min(T, 256)
    nb = T // tb
    return pl.pallas_call(
        functools.partial(_dn_kernel, tb=tb, nb=nb),
        grid=(B, nb),
        in_specs=[pl.BlockSpec((tb, DN_QKV), lambda b_, t: (b_ * nb + t, P_QKV // DN_QKV)),
                  pl.BlockSpec((tb, 512), lambda b_, t: (b_ * nb + t, P_Z // 512)),
                  pl.BlockSpec((tb, LANES), lambda b_, t: (b_ * nb + t, 0)),
                  pl.BlockSpec((1, DN_CONV - 1, DN_QKV), lambda b_, t: (b_, 0, 0)),
                  pl.BlockSpec((DN_CONV, DN_QKV), lambda b_, t: (0, 0)),
                  pl.BlockSpec((1, LANES), lambda b_, t: (0, 0)),
                  pl.BlockSpec((1, LANES), lambda b_, t: (0, 0)),
                  pl.BlockSpec((1, DN_DV), lambda b_, t: (0, 0)),
                  pl.BlockSpec((1, DN_HEADS, DN_DK, DN_DV), lambda b_, t: (b_, 0, 0, 0))],
        out_specs=[pl.BlockSpec((tb, DN_HEADS * DN_DV), lambda b_, t: (b_ * nb + t, 0)),
                   pl.BlockSpec((1, DN_CONV - 1, DN_QKV), lambda b_, t: (b_, 0, 0)),
                   pl.BlockSpec((1, DN_HEADS, DN_DK, DN_DV), lambda b_, t: (b_, 0, 0, 0))],
        out_shape=[SDS((B * T, DN_HEADS * DN_DV), act_dtype),
                   SDS((B, DN_CONV - 1, DN_QKV), F32),
                   SDS((B, DN_HEADS, DN_DK, DN_DV), F32)],
        scratch_shapes=[pltpu.VMEM((tb + 8, DN_QKV), F32),
                        pltpu.VMEM((DN_DK, DN_HEADS * DN_DV), F32)],
        compiler_params=_cparams("parallel", "arbitrary"),
        name="deltanet",
    )(p, p, ba, prev, cw, avec, dvec, ng, s0)


def _qknorm_kernel(q_ref, k_ref, v_ref, gq_ref, gk_ref, c_ref, s1_ref, s2_ref,
                   qo_ref, kf_ref, kb_ref, vb_ref):
    ri = lax.broadcasted_iota(jnp.int32, (LANES, LANES), 0)
    ci = lax.broadcasted_iota(jnp.int32, (LANES, LANES), 1)
    seg = ((ri // DIFF_DQK) == (ci // DIFF_DQK)).astype(BF16)
    cos = c_ref[...]
    s1 = s1_ref[...]
    s2 = s2_ref[...]

    def norm_rope(x, g):
        sq = x * x
        hi = sq.astype(BF16)
        r1 = sq - hi.astype(F32)
        mid = r1.astype(BF16)
        lo = (r1 - mid.astype(F32)).astype(BF16)
        tot = _dot(hi, seg) + (_dot(mid, seg) + _dot(lo, seg))
        y = x * lax.rsqrt(tot * (1.0 / DIFF_DQK) + EPS) * g
        return y * cos + pltpu.roll(y, LANES - ROT_DIM // 2, 1) * s1 + pltpu.roll(y, ROT_DIM // 2, 1) * s2

    for c in range(2 * DIFF_HEADS * DIFF_DQK // LANES):
        sl = slice(c * LANES, (c + 1) * LANES)
        qn = norm_rope(q_ref[:, sl], gq_ref[...])
        kn = norm_rope(k_ref[:, sl], gk_ref[...])
        qo_ref[:, sl] = (qn * (DIFF_DQK ** -0.5)).astype(qo_ref.dtype)
        kf_ref[:, sl] = kn
        kb_ref[:, sl] = kn.astype(kb_ref.dtype)
    vb_ref[...] = v_ref[...].astype(vb_ref.dtype)


def _qknorm(p, gq, gk, cos, s1, s2, n, T, act_dtype):
    tm = min(T, 512) if T % 16 == 0 else n
    nrep = T // tm if T % 16 == 0 else 1
    W = 1024
    tab = pl.BlockSpec((tm, LANES), lambda i: (i % nrep, 0))
    vec = pl.BlockSpec((1, LANES), lambda i: (0, 0))
    return pl.pallas_call(
        _qknorm_kernel,
        grid=(n // tm,),
        in_specs=[pl.BlockSpec((tm, W), lambda i: (i, P_DQ // W)),
                  pl.BlockSpec((tm, W), lambda i: (i, P_DK // W)),
                  pl.BlockSpec((tm, W), lambda i: (i, P_DV // W)),
                  vec, vec, tab, tab, tab],
        out_specs=[pl.BlockSpec((tm, W), lambda i: (i, 0))] * 4,
        out_shape=[SDS((n, W), act_dtype), SDS((n, W), F32), SDS((n, W), act_dtype), SDS((n, W), act_dtype)],
        compiler_params=_cparams("parallel"),
        name="qknorm_rope",
    )(p, p, p, gq, gk, cos, s1, s2)


def _rope_tables(pos):
    half = ROT_DIM // 2
    inv = ROPE_THETA ** (-jnp.arange(0, ROT_DIM, 2, dtype=F32) / ROT_DIM)
    ang = pos.astype(F32)[:, None] * inv[None, :]
    cos, sin = jnp.cos(ang), jnp.sin(ang)
    T = pos.shape[0]
    one = jnp.ones((T, DIFF_DQK - ROT_DIM), F32)
    zero = jnp.zeros((T, DIFF_DQK - ROT_DIM), F32)
    zh = jnp.zeros((T, half), F32)
    c64 = jnp.concatenate([cos, cos, one], axis=1)
    s1_64 = jnp.concatenate([-sin, zh, zero], axis=1)
    s2_64 = jnp.concatenate([zh, sin, zero], axis=1)
    return tuple(jnp.concatenate([a, a], axis=1) for a in (c64, s1_64, s2_64))


def _lambda(l1q, l1k, l2q, l2k, lam_init):
    a = jnp.sum(l1q * l1k, axis=-1, keepdims=True)
    b = jnp.sum(l2q * l2k, axis=-1, keepdims=True)
    return jnp.exp(a) - jnp.exp(b) + lam_init


def _diff_finish(acc0, l0, acc1, l1, lam, g, lam_init):
    o = acc0 / l0 - lam * (acc1 / l1)
    o = o * lax.rsqrt(jnp.mean(o * o, axis=-1, keepdims=True) + EPS) * g
    return o * (1.0 - lam_init)


def _flash_kernel(q_ref, k_ref, v_ref, l1q_ref, l1k_ref, l2q_ref, l2k_ref, g_ref, o_ref, *, tq, lam_init):
    qi = pl.program_id(2)
    q = q_ref[...]
    lane = lax.broadcasted_iota(jnp.int32, (1, LANES), 1)
    qs = [jnp.where((lane // DIFF_DQK) == j, q, jnp.zeros_like(q)) for j in range(2)]

    def block(off, carry, keep):
        k = k_ref[pl.ds(off, tq), :]
        v = v_ref[pl.ds(off, tq), :]
        ss = [_dot_nt(qs[j], k) for j in range(2)]
        out = []
        for j in range(2):
            m_prev, l_prev, a_prev = carry[j]
            s = ss[j] if keep is None else jnp.where(keep, ss[j], NEG)
            m_new = jnp.maximum(m_prev, jnp.max(s, axis=-1, keepdims=True))
            alpha = jnp.exp(m_prev - m_new)
            p = jnp.exp(s - m_new)
            l_new = alpha * l_prev + jnp.sum(p, axis=-1, keepdims=True)
            a_new = alpha * a_prev + _dot(p.astype(BF16), v)
            out.append((m_new, l_new, a_new))
        return tuple(out)

    one = (jnp.full((tq, 1), NEG, F32), jnp.zeros((tq, 1), F32), jnp.zeros((tq, DIFF_DV), F32))
    carry = lax.fori_loop(0, qi, lambda ki, c: block(pl.multiple_of(ki * tq, tq), c, None), (one, one))
    rr = lax.broadcasted_iota(jnp.int32, (tq, tq), 0)
    cc = lax.broadcasted_iota(jnp.int32, (tq, tq), 1)
    (m0, l0, a0), (m1, l1, a1) = block(pl.multiple_of(qi * tq, tq), carry, cc <= rr)
    del m0, m1
    lam = _lambda(l1q_ref[...], l1k_ref[...], l2q_ref[...], l2k_ref[...], lam_init)
    o_ref[...] = _diff_finish(a0, l0, a1, l1, lam, g_ref[...], lam_init).astype(o_ref.dtype)


def _flash_attention(q, k, v, lparams, g, B, T, lam_init):
    tq = min(T, 512)
    nq = T // tq
    vec64 = pl.BlockSpec((1, DIFF_DQK), lambda b_, h, qi: (0, 0))
    kv = pl.BlockSpec((T, LANES), lambda b_, h, qi: (b_, h))
    return pl.pallas_call(
        functools.partial(_flash_kernel, tq=tq, lam_init=lam_init),
        grid=(B, DIFF_HEADS, nq),
        in_specs=[pl.BlockSpec((tq, LANES), lambda b_, h, qi: (b_ * nq + qi, h)), kv, kv,
                  vec64, vec64, vec64, vec64,
                  pl.BlockSpec((1, DIFF_DV), lambda b_, h, qi: (0, 0))],
        out_specs=pl.BlockSpec((tq, LANES), lambda b_, h, qi: (b_ * nq + qi, h)),
        out_shape=SDS((B * T, DIFF_HEADS * DIFF_DV), BF16),
        compiler_params=_cparams("parallel", "parallel", "arbitrary"),
        name="flash_diff_attention",
    )(q, k, v, *lparams, g)


def _paged_kernel(pt_ref, q_ref, kn_ref, vn_ref, l1q_ref, l1k_ref, l2q_ref, l2k_ref, g_ref, e_ref, *rest,
                  pps, nsteps, tnew, lam_init):
    del pt_ref
    kv_refs = rest[:2 * pps]
    o_ref, m_sc, l_sc, acc_sc = rest[2 * pps:]
    s_id = pl.program_id(1)
    nsub = 2 * DIFF_HEADS
    nrow = nsub * tnew
    kdim = nsub * DIFF_DQK

    @pl.when(s_id == 0)
    def _():
        m_sc[...] = jnp.full(m_sc.shape, NEG, F32)
        l_sc[...] = jnp.zeros(l_sc.shape, F32)
        acc_sc[...] = jnp.zeros(acc_sc.shape, F32)

    ntok = kn_ref.shape[-1]
    er = lax.broadcasted_iota(jnp.int32, (nrow, ntok * DIFF_HEADS), 0)
    ec = lax.broadcasted_iota(jnp.int32, (nrow, ntok * DIFF_HEADS), 1)
    own_head = (ec % DIFF_HEADS) == (er // (2 * tnew))

    def update(k_refs, v_refs, keep):
        ss = [_dot(q_ref[0], k[...].reshape(kdim, ntok).astype(BF16)) for k in k_refs]
        if keep is not None:
            ss = [jnp.where(keep, s, NEG) for s in ss]
        m_prev = m_sc[...]
        m_new = m_prev
        for s in ss:
            m_new = jnp.maximum(m_new, jnp.max(s, axis=-1, keepdims=True))
        alpha = jnp.exp(m_prev - m_new)
        l_new = alpha * l_sc[...]
        acc = alpha * acc_sc[...]
        for s, v in zip(ss, v_refs):
            p = jnp.exp(s - m_new)
            if keep is not None:
                p = jnp.where(keep, p, 0.0)
            l_new = l_new + jnp.sum(p, axis=-1, keepdims=True)
            p_wide = _dot(p.astype(BF16), e_ref[...])
            p_bd = jnp.where(own_head, p_wide, 0.0).astype(BF16)
            acc = acc + _dot(p_bd, v[...].reshape(ntok * DIFF_HEADS, DIFF_DV).astype(BF16))
        l_sc[...] = l_new
        acc_sc[...] = acc
        m_sc[...] = m_new

    @pl.when(s_id < nsteps - 1)
    def _():
        update(kv_refs[:pps], kv_refs[pps:], None)

    @pl.when(s_id == nsteps - 1)
    def _():
        rr = lax.broadcasted_iota(jnp.int32, (nrow, ntok), 0)
        cc = lax.broadcasted_iota(jnp.int32, (nrow, ntok), 1)
        update([kn_ref.at[0]], [vn_ref.at[0]], cc <= (rr % tnew))
        lam = _lambda(l1q_ref[...], l1k_ref[...], l2q_ref[...], l2k_ref[...], lam_init)
        o = acc_sc[...] / l_sc[...]
        rows = 2 * tnew
        for hv in range(DIFF_HEADS):
            o1 = o[hv * rows:hv * rows + tnew]
            o2 = o[hv * rows + tnew:(hv + 1) * rows]
            d = o1 - lam * o2
            d = d * lax.rsqrt(jnp.mean(d * d, axis=-1, keepdims=True) + EPS) * g_ref[...]
            o_ref[0, hv * tnew:(hv + 1) * tnew, :] = d * (1.0 - lam_init)


def _paged_attention(q_bd, k_new_t, v_new, cache_kt, cache_v, page_table, layer, lparams, g, lam_init, tnew):
    DB, n_pages = page_table.shape
    page = cache_v.shape[2]
    pps = 8 if n_pages % 8 == 0 else 1
    nsteps = n_pages // pps + 1
    nsub = 2 * DIFF_HEADS
    nrow = nsub * tnew

    def kv_spec(r, shape):
        def imap(b_, s, pt):
            idx = jnp.minimum(s * pps + r, n_pages - 1)
            return (layer, pt[b_ * n_pages + idx], 0, 0, 0)
        return pl.BlockSpec((None, None) + shape, imap)

    vec64 = pl.BlockSpec((1, DIFF_DQK), lambda b_, s, pt: (0, 0))
    in_specs = [pl.BlockSpec((1, nrow, nsub * DIFF_DQK), lambda b_, s, pt: (b_, 0, 0)),
                pl.BlockSpec((1, nsub, DIFF_DQK, page), lambda b_, s, pt: (b_, 0, 0, 0)),
                pl.BlockSpec((1, page, DIFF_HEADS, DIFF_DV), lambda b_, s, pt: (b_, 0, 0, 0)),
                vec64, vec64, vec64, vec64,
                pl.BlockSpec((1, DIFF_DV), lambda b_, s, pt: (0, 0)),
                pl.BlockSpec((page, page * DIFF_HEADS), lambda b_, s, pt: (0, 0))]
    spread = (jnp.arange(page * DIFF_HEADS)[None, :] // DIFF_HEADS == jnp.arange(page)[:, None]).astype(BF16)
    in_specs += [kv_spec(r, (nsub, DIFF_DQK, page)) for r in range(pps)]
    in_specs += [kv_spec(r, (page, DIFF_HEADS, DIFF_DV)) for r in range(pps)]
    grid_spec = pltpu.PrefetchScalarGridSpec(
        num_scalar_prefetch=1,
        grid=(DB, nsteps),
        in_specs=in_specs,
        out_specs=pl.BlockSpec((1, DIFF_HEADS * tnew, DIFF_DV), lambda b_, s, pt: (b_, 0, 0)),
        scratch_shapes=[pltpu.VMEM((nrow, 1), F32), pltpu.VMEM((nrow, 1), F32),
                        pltpu.VMEM((nrow, DIFF_DV), F32)])
    return pl.pallas_call(
        functools.partial(_paged_kernel, pps=pps, nsteps=nsteps, tnew=tnew, lam_init=lam_init),
        grid_spec=grid_spec,
        out_shape=SDS((DB, DIFF_HEADS * tnew, DIFF_DV), F32),
        compiler_params=_cparams("parallel", "arbitrary"),
        name="paged_diff_attention",
    )(page_table.reshape(-1), q_bd, k_new_t, v_new, *lparams, g, spread,
      *([cache_kt] * pps), *([cache_v] * pps))


def _outproj_kernel(a_ref, b_ref, c_ref, wa_ref, wb_ref, wc_ref, x_ref, o_ref):
    acc = _dot(a_ref[...].astype(BF16), wa_ref[...])
    acc = acc + _dot(b_ref[...].astype(BF16), wb_ref[...])
    acc = acc + _dot(c_ref[...].astype(BF16), wc_ref[...])
    o_ref[...] = x_ref[...] + acc


def _out_proj(conf, odn, att, w_out, x):
    n = x.shape[0]
    tm = min(n, 1024)
    tn = 512
    return pl.pallas_call(
        _outproj_kernel,
        grid=(n // tm, D_MODEL // tn),
        in_specs=[pl.BlockSpec((tm, 512), lambda i, j: (i, 0)),
                  pl.BlockSpec((tm, 512), lambda i, j: (i, 0)),
                  pl.BlockSpec((tm, 1024), lambda i, j: (i, 0)),
                  pl.BlockSpec((512, tn), lambda i, j: (0, j)),
                  pl.BlockSpec((512, tn), lambda i, j: (1, j)),
                  pl.BlockSpec((1024, tn), lambda i, j: (1, j)),
                  pl.BlockSpec((tm, tn), lambda i, j: (i, j))],
        out_specs=pl.BlockSpec((tm, tn), lambda i, j: (i, j)),
        out_shape=SDS((n, D_MODEL), F32),
        compiler_params=_cparams("parallel", "arbitrary"),
        name="out_proj",
    )(conf, odn, att, w_out, w_out, w_out, x)


def _router_kernel(x_ref, g_ref, wr_ref, br_ref, h_ref, gate_ref):
    x = x_ref[...]
    y = x * lax.rsqrt(jnp.mean(x * x, axis=-1, keepdims=True) + EPS)
    h = y * g_ref[...]
    h_ref[...] = h.astype(h_ref.dtype)
    logits = _mm3(h, wr_ref[...]) + br_ref[...]
    lane = lax.broadcasted_iota(jnp.int32, logits.shape, 1)
    big = jnp.int32(1 << 20)
    is_g = (lane >= N_EXPERTS) & (lane < N_EXPERTS + N_GROUPS)
    gl = jnp.where(is_g, logits, NEG)
    gmax = jnp.max(gl, axis=-1, keepdims=True)
    gsel = jnp.min(jnp.where(is_g & (gl == gmax), lane, big), axis=-1, keepdims=True) - N_EXPERTS
    gw = 1.0 / jnp.sum(jnp.where(is_g, jnp.exp(gl - gmax), 0.0), axis=-1, keepdims=True)
    in_grp = (lane >= gsel * EXPERTS_PER_GROUP) & (lane < (gsel + 1) * EXPERTS_PER_GROUP)
    el = jnp.where(in_grp, logits, NEG)
    v1 = jnp.max(el, axis=-1, keepdims=True)
    i1 = jnp.min(jnp.where(in_grp & (el == v1), lane, big), axis=-1, keepdims=True)
    rest = in_grp & (lane != i1)
    el2 = jnp.where(rest, logits, NEG)
    v2 = jnp.max(el2, axis=-1, keepdims=True)
    i2 = jnp.min(jnp.where(rest & (el2 == v2), lane, big), axis=-1, keepdims=True)
    e2 = jnp.exp(v2 - v1)
    den = 1.0 + e2
    gate_ref[...] = jnp.where(lane == i1, (1.0 / den) * gw, 0.0) + jnp.where(lane == i2, (e2 / den) * gw, 0.0)


def _router(x, g, wr, br, act_dtype):
    n = x.shape[0]
    tm = min(n, 512)
    return pl.pallas_call(
        _router_kernel,
        grid=(n // tm,),
        in_specs=[pl.BlockSpec((tm, D_MODEL), lambda i: (i, 0)),
                  pl.BlockSpec((1, D_MODEL), lambda i: (0, 0)),
                  pl.BlockSpec((D_MODEL, LANES), lambda i: (0, 0)),
                  pl.BlockSpec((1, LANES), lambda i: (0, 0))],
        out_specs=[pl.BlockSpec((tm, D_MODEL), lambda i: (i, 0)),
                   pl.BlockSpec((tm, LANES), lambda i: (i, 0))],
        out_shape=[SDS((n, D_MODEL), act_dtype), SDS((n, LANES), F32)],
        compiler_params=_cparams("parallel"),
        name="norm_router",
    )(x, g, wr, br)


def _moe_kernel(h_ref, gate_ref, x_ref, wg_ref, wu_ref, wd_ref, o_ref):
    e = pl.program_id(1)

    @pl.when(e == 0)
    def _():
        o_ref[...] = x_ref[...]

    h = h_ref[...].astype(BF16)
    hg = _dot(h, wg_ref[...])
    hu = _dot(h, wu_ref[...])
    lane = lax.broadcasted_iota(jnp.int32, gate_ref.shape, 1)
    ge = jnp.sum(jnp.where(lane == e, gate_ref[...], 0.0), axis=-1, keepdims=True)
    act = _silu(hg) * hu * ge
    o_ref[...] += _dot(act.astype(BF16), wd_ref[...])


def _moe(h, gates, x, wg, wu, wd):
    n = x.shape[0]
    tm = min(n, 512)
    return pl.pallas_call(
        _moe_kernel,
        grid=(n // tm, N_EXPERTS),
        in_specs=[pl.BlockSpec((tm, D_MODEL), lambda i, e: (i, 0)),
                  pl.BlockSpec((tm, LANES), lambda i, e: (i, 0)),
                  pl.BlockSpec((tm, D_MODEL), lambda i, e: (i, 0)),
                  pl.BlockSpec((None, D_MODEL, D_EXPERT), lambda i, e: (e, 0, 0)),
                  pl.BlockSpec((None, D_MODEL, D_EXPERT), lambda i, e: (e, 0, 0)),
                  pl.BlockSpec((None, D_EXPERT, D_MODEL), lambda i, e: (e, 0, 0))],
        out_specs=pl.BlockSpec((tm, D_MODEL), lambda i, e: (i, 0)),
        out_shape=SDS((n, D_MODEL), F32),
        compiler_params=_cparams("parallel", "arbitrary"),
        name="moe_experts",
    )(h, gates, x, wg, wu, wd)


def _row(v, width=None):
    v = v.astype(F32).reshape(1, -1)
    if width is not None and v.shape[1] < width:
        v = jnp.pad(v, ((0, 0), (0, width - v.shape[1])))
    return v


def _prep_layer(l, w):
    wi = w['w_in'][l]
    w_main = jnp.concatenate([wi[:, 1024:2560], wi[:, 2560:3072], wi[:, 0:1024], wi[:, 3080:6152]],
                             axis=1).astype(BF16)
    w_ba = jnp.pad(wi[:, 3072:3080], ((0, 0), (0, LANES - 8))).astype(BF16)
    wr = jnp.concatenate([w['router_expert_w'][l], w['router_group_w'][l]], axis=1)
    wr = jnp.pad(wr, ((0, 0), (0, LANES - wr.shape[1])))
    br = _row(jnp.concatenate([w['router_expert_b'][l], w['router_group_b'][l]]), LANES)
    lane_pad = lambda v: _row(jnp.concatenate([jnp.zeros((DN_HEADS,), F32), v.astype(F32)]), LANES)
    return dict(
        norm1_g=_row(w['norm1_g'][l]), w_main=w_main, w_ba=w_ba,
        conf_w=w['conf_dw_w'][l], conf_b=_row(w['conf_dw_b'][l]),
        conf_g=_row(w['conf_ln_g'][l]), conf_bb=_row(w['conf_ln_b'][l]),
        dn_cw=w['dn_conv_w'][l], dn_av=lane_pad(w['dn_a_log'][l]), dn_dv=lane_pad(w['dn_dt_bias'][l]),
        dn_ng=_row(w['dn_norm_g'][l]),
        gq=_row(jnp.tile(w['diff_q_norm_g'][l], 2)), gk=_row(jnp.tile(w['diff_k_norm_g'][l], 2)),
        lparams=tuple(_row(w[k][l]) for k in ('diff_lambda_q1', 'diff_lambda_k1',
                                               'diff_lambda_q2', 'diff_lambda_k2')),
        subln=_row(w['diff_subln_g'][l]),
        w_out=w['w_out'][l].astype(BF16), norm2_g=_row(w['norm2_g'][l]), wr=wr, br=br,
        wg=w['moe_w_gate'][l].astype(BF16), wu=w['moe_w_up'][l].astype(BF16),
        wd=w['moe_w_down'][l].astype(BF16))


def _layer(x, B, T, tables, conf_prev, dnc_prev, dn_s0, lp, lam_init, paged):
    n = B * T
    act_dtype = BF16 if T % 16 == 0 else F32
    p, ba = _in_proj(x, lp['norm1_g'], lp['w_main'], lp['w_ba'])
    conf, conf_new = _conformer(p, conf_prev, lp['conf_w'], lp['conf_b'], lp['conf_g'], lp['conf_bb'],
                                B, T, act_dtype)
    odn, dnc_new, dn_state = _deltanet(p, ba, dnc_prev, lp['dn_cw'], lp['dn_av'], lp['dn_dv'],
                                       lp['dn_ng'], dn_s0, B, T, act_dtype)
    qn, kf, kb, vb = _qknorm(p, lp['gq'], lp['gk'], *tables, n, T, act_dtype)
    v_f32 = p[:, P_DV:P_DV + DIFF_HEADS * DIFF_DV]
    k_out = kf.reshape(B, T, 2 * DIFF_HEADS, DIFF_DQK)
    v_out = v_f32.reshape(B, T, DIFF_HEADS, DIFF_DV)
    if paged is None:
        att = _flash_attention(qn, kb, vb, lp['lparams'], lp['subln'], B, T, lam_init)
    else:
        cache_kt, cache_v, page_table, layer = paged
        page = cache_v.shape[2]
        nsub = 2 * DIFF_HEADS
        q4 = qn.reshape(B, T, nsub, DIFF_DQK).transpose(0, 2, 1, 3)
        eye = jnp.eye(nsub, dtype=qn.dtype)
        q_bd = (q4[:, :, :, None, :] * eye[None, :, None, :, None]).reshape(B, nsub * T, nsub * DIFF_DQK)
        k_new_t = jnp.pad(k_out.transpose(0, 2, 3, 1), ((0, 0), (0, 0), (0, 0), (0, page - T)))
        v_new = jnp.pad(v_out, ((0, 0), (0, page - T), (0, 0), (0, 0)))
        att = _paged_attention(q_bd.astype(BF16), k_new_t, v_new, cache_kt, cache_v, page_table, layer,
                               lp['lparams'], lp['subln'], lam_init, T)
        att = att.reshape(B, DIFF_HEADS, T, DIFF_DV).transpose(0, 2, 1, 3).reshape(n, DIFF_HEADS * DIFF_DV)
    x2 = _out_proj(conf, odn, att, lp['w_out'], x)
    h2, gates = _router(x2, lp['norm2_g'], lp['wr'], lp['br'], act_dtype)
    x3 = _moe(h2, gates, x2, lp['wg'], lp['wu'], lp['wd'])
    return x3, (k_out, v_out, dn_state, dnc_new, conf_new)


def kernel(x_prompt, x_sample, cache_k, cache_v, page_table, state_delta, state_delta_conv, state_conf_conv, norm1_g, w_in, conf_dw_w, conf_dw_b, conf_ln_g, conf_ln_b, dn_conv_w, dn_a_log, dn_dt_bias, dn_norm_g, diff_q_norm_g, diff_k_norm_g, diff_lambda_q1, diff_lambda_k1, diff_lambda_q2, diff_lambda_k2, diff_subln_g, w_out, norm2_g, router_group_w, router_group_b, router_expert_w, router_expert_b, moe_w_gate, moe_w_up, moe_w_down):
    w = dict(norm1_g=norm1_g, w_in=w_in, conf_dw_w=conf_dw_w, conf_dw_b=conf_dw_b, conf_ln_g=conf_ln_g,
             conf_ln_b=conf_ln_b, dn_conv_w=dn_conv_w, dn_a_log=dn_a_log, dn_dt_bias=dn_dt_bias,
             dn_norm_g=dn_norm_g, diff_q_norm_g=diff_q_norm_g, diff_k_norm_g=diff_k_norm_g,
             diff_lambda_q1=diff_lambda_q1, diff_lambda_k1=diff_lambda_k1, diff_lambda_q2=diff_lambda_q2,
             diff_lambda_k2=diff_lambda_k2, diff_subln_g=diff_subln_g, w_out=w_out, norm2_g=norm2_g,
             router_group_w=router_group_w, router_group_b=router_group_b,
             router_expert_w=router_expert_w, router_expert_b=router_expert_b,
             moe_w_gate=moe_w_gate, moe_w_up=moe_w_up, moe_w_down=moe_w_down)
    B, S, _ = x_prompt.shape
    DB, T, _ = x_sample.shape
    depth = w_in.shape[0]
    past_len = page_table.shape[1] * cache_k.shape[2]
    cache_kt = jnp.transpose(cache_k, (0, 1, 3, 4, 2))
    tab_p = _rope_tables(jnp.arange(S, dtype=jnp.int32))
    tab_s = tuple(jnp.tile(a, (DB, 1)) for a in _rope_tables(past_len + jnp.arange(T, dtype=jnp.int32)))
    hp = x_prompt.reshape(B * S, D_MODEL)
    hs = x_sample.reshape(DB * T, D_MODEL)
    zero_conf = jnp.zeros((B, CONF_K - 1, CONF_W), F32)
    zero_dnc = jnp.zeros((B, DN_CONV - 1, DN_QKV), F32)
    zero_s = jnp.zeros((B, DN_HEADS, DN_DK, DN_DV), F32)
    outs_p, outs_s = [], []
    for l in range(depth):
        lp = _prep_layer(l, w)
        lam_init = 0.8 - 0.6 * math.exp(-0.3 * l)
        hp, sp = _layer(hp, B, S, tab_p, zero_conf, zero_dnc, zero_s, lp, lam_init, None)
        hs, ss = _layer(hs, DB, T, tab_s, state_conf_conv[l], state_delta_conv[l], state_delta[l], lp,
                        lam_init, (cache_kt, cache_v, page_table, l))
        outs_p.append(sp)
        outs_s.append(ss)
    st = lambda outs, i: jnp.stack([o[i] for o in outs])
    return (hp.reshape(B, S, D_MODEL), hs.reshape(DB, T, D_MODEL),
            st(outs_p, 0), st(outs_p, 1), st(outs_s, 0), st(outs_s, 1),
            st(outs_p, 2), st(outs_s, 2), st(outs_p, 3), st(outs_s, 3),
            st(outs_p, 4), st(outs_s, 4))
```

```python
import functools
import math

import jax
import jax.numpy as jnp
from jax import lax
from jax.experimental import pallas as pl
from jax.experimental.pallas import tpu as pltpu

F32 = jnp.float32
BF16 = jnp.bfloat16
SDS = jax.ShapeDtypeStruct

D_MODEL = 2048
CONF_W = 512
CONF_K = 31
DN_HEADS = 4
DN_DK = 128
DN_DV = 128
DN_CONV = 4
DN_CHUNK = 64
DN_QKV = DN_HEADS * (2 * DN_DK + DN_DV)
DIFF_HEADS = 8
DIFF_DQK = 64
DIFF_DV = 128
ROT_DIM = 16
ROPE_THETA = 500000.0
N_GROUPS = 4
EXPERTS_PER_GROUP = 4
N_EXPERTS = 16
D_EXPERT = 512
EPS = 1e-6
LANES = 128
VMEM_LIMIT = 48 * 1024 * 1024

P_QKV, P_Z, P_GLU, P_DQ, P_DK, P_DV = 0, 1536, 2048, 3072, 4096, 5120
P_W = 6144
NEG = -1e30


def _cparams(*sem):
    return pltpu.CompilerParams(dimension_semantics=sem, vmem_limit_bytes=VMEM_LIMIT)


def _dot(a, b):
    return jnp.dot(a, b, preferred_element_type=F32)


def _dot_nt(a, b):
    return lax.dot_general(a, b, (((1,), (1,)), ((), ())), preferred_element_type=F32)


def _split(a):
    hi = a.astype(BF16)
    return hi, (a - hi.astype(F32)).astype(BF16)


def _mm3(a, b):
    ah, al = _split(a)
    bh, bl = _split(b)
    return _dot(ah, bh) + (_dot(ah, bl) + _dot(al, bh))


def _sigmoid(x):
    return 1.0 / (1.0 + jnp.exp(-x))


def _silu(x):
    return x * _sigmoid(x)


def _softplus(x):
    return jnp.maximum(x, 0.0) + jnp.log1p(jnp.exp(-jnp.abs(x)))


def _inproj_kernel(x_ref, g_ref, w_ref, wba_ref, o_ref, oba_ref, h_ref):
    @pl.when(pl.program_id(1) == 0)
    def _():
        x = x_ref[...]
        y = x * lax.rsqrt(jnp.mean(x * x, axis=-1, keepdims=True) + EPS)
        h_ref[...] = (y * g_ref[...]).astype(BF16)
        oba_ref[...] = _dot(h_ref[...], wba_ref[...])

    o_ref[...] = _dot(h_ref[...], w_ref[...])


def _in_proj(x, g, w_main, w_ba):
    n = x.shape[0]
    tm = min(n, 1024)
    tn = 512
    return pl.pallas_call(
        _inproj_kernel,
        grid=(n // tm, P_W // tn),
        in_specs=[pl.BlockSpec((tm, D_MODEL), lambda i, j: (i, 0)),
                  pl.BlockSpec((1, D_MODEL), lambda i, j: (0, 0)),
                  pl.BlockSpec((D_MODEL, tn), lambda i, j: (0, j)),
                  pl.BlockSpec((D_MODEL, LANES), lambda i, j: (0, 0))],
        out_specs=[pl.BlockSpec((tm, tn), lambda i, j: (i, j)),
                   pl.BlockSpec((tm, LANES), lambda i, j: (i, 0))],
        out_shape=[SDS((n, P_W), F32), SDS((n, LANES), F32)],
        scratch_shapes=[pltpu.VMEM((tm, D_MODEL), BF16)],
        compiler_params=_cparams("parallel", "arbitrary"),
        name="in_proj",
    )(x, g, w_main, w_ba)


def _conf_kernel(p_ref, prev_ref, w_ref, b_ref, g_ref, bb_ref, o_ref, new_ref, ubuf, *, tt, nt):
    t = pl.program_id(1)
    hist = 32

    @pl.when(t == 0)
    def _():
        ubuf[0:2, :] = jnp.zeros((2, CONF_W), F32)
        ubuf[2:hist, :] = prev_ref[0]

    if nt > 1:
        @pl.when(t > 0)
        def _():
            ubuf[0:hist, :] = ubuf[tt:tt + hist, :]

    x = p_ref[...]
    ubuf[hist:hist + tt, :] = x[:, :CONF_W] * _sigmoid(x[:, CONF_W:])
    rc = min(tt, 64)
    for r0 in range(0, tt, rc):
        acc = jnp.zeros((rc, CONF_W), F32)
        for j in range(CONF_K):
            s = r0 + hist - (CONF_K - 1) + j
            acc = acc + w_ref[j:j + 1, :] * ubuf[s:s + rc, :]
        c = acc + b_ref[...]
        xc = c - jnp.mean(c, axis=-1, keepdims=True)
        y = xc * lax.rsqrt(jnp.mean(xc * xc, axis=-1, keepdims=True) + EPS)
        y = y * g_ref[...] + bb_ref[...]
        o_ref[r0:r0 + rc, :] = _silu(y).astype(o_ref.dtype)

    @pl.when(t == nt - 1)
    def _():
        new_ref[0] = ubuf[tt + hist - (CONF_K - 1):tt + hist, :]


def _conformer(p, prev, w, b, g, bb, B, T, act_dtype):
    tt = min(T, 256)
    nt = T // tt
    return pl.pallas_call(
        functools.partial(_conf_kernel, tt=tt, nt=nt),
        grid=(B, nt),
        in_specs=[pl.BlockSpec((tt, 2 * CONF_W), lambda b_, t: (b_ * nt + t, P_GLU // (2 * CONF_W))),
                  pl.BlockSpec((1, CONF_K - 1, CONF_W), lambda b_, t: (b_, 0, 0)),
                  pl.BlockSpec((CONF_K, CONF_W), lambda b_, t: (0, 0)),
                  pl.BlockSpec((1, CONF_W), lambda b_, t: (0, 0)),
                  pl.BlockSpec((1, CONF_W), lambda b_, t: (0, 0)),
                  pl.BlockSpec((1, CONF_W), lambda b_, t: (0, 0))],
        out_specs=[pl.BlockSpec((tt, CONF_W), lambda b_, t: (b_ * nt + t, 0)),
                   pl.BlockSpec((1, CONF_K - 1, CONF_W), lambda b_, t: (b_, 0, 0))],
        out_shape=[SDS((B * T, CONF_W), act_dtype), SDS((B, CONF_K - 1, CONF_W), F32)],
        scratch_shapes=[pltpu.VMEM((tt + 32, CONF_W), F32)],
        compiler_params=_cparams("parallel", "arbitrary"),
        name="conformer",
    )(p, prev, w, b, g, bb)


def _stack_heads(a):
    return jnp.concatenate([a[:, h * LANES:(h + 1) * LANES] for h in range(DN_HEADS)], axis=0)


def _unstack_heads(a, c):
    return jnp.concatenate([a[h * c:(h + 1) * c, :] for h in range(DN_HEADS)], axis=1)


def _stack_cols(a, lane0):
    return jnp.concatenate([a[:, lane0 + h:lane0 + h + 1] for h in range(DN_HEADS)], axis=0)


def _diag_blocks(x, rowhead):
    out = jnp.where(rowhead == 0, x[:, 0:LANES], 0.0)
    for h in range(1, DN_HEADS):
        out = out + jnp.where(rowhead == h, x[:, h * LANES:(h + 1) * LANES], 0.0)
    return out


def _unit_lower_inverse(lmat, ri, ci):
    mm1 = lambda a, b: _dot(a.astype(BF16), b.astype(BF16))
    eye = (ri == ci).astype(F32)
    d = jnp.where((ri // 16) == (ci // 16), lmat, 0.0)
    d2 = mm1(d, d)
    d4 = mm1(d2, d2)
    d8 = mm1(d4, d4)
    t = mm1(mm1(mm1(eye - d, eye + d2), eye + d4), eye + d8)
    m32 = jnp.where(((ri // 32) == (ci // 32)) & ((ri // 16) != (ci // 16)), lmat, 0.0)
    t = t - mm1(mm1(t, m32), t)
    m64 = jnp.where((ri // 32) != (ci // 32), lmat, 0.0)
    t = t - mm1(mm1(t, m64), t)
    resid = eye - t - _mm3(lmat, t)
    return t + mm1(t, resid)


def _dn_kernel(qkv_ref, z_ref, ba_ref, prev_ref, cw_ref, av_ref, dv_ref, ng_ref, s0_ref,
               o_ref, cnew_ref, snew_ref, cbuf, s_sc, *, tb, nb):
    t = pl.program_id(1)
    C = DN_CHUNK
    H = DN_HEADS
    HC = H * C
    npre = DN_CONV - 1

    @pl.when(t == 0)
    def _():
        cbuf[0:8 - npre, :] = jnp.zeros((8 - npre, DN_QKV), F32)
        cbuf[8 - npre:8, :] = prev_ref[0]
        s_sc[...] = jnp.concatenate([s0_ref[0, h] for h in range(H)], axis=1)

    if nb > 1:
        @pl.when(t > 0)
        def _():
            cbuf[0:8, :] = cbuf[tb:tb + 8, :]

    cbuf[8:8 + tb, :] = qkv_ref[...]
    y = jnp.zeros((tb, DN_QKV), F32)
    for j in range(DN_CONV):
        y = y + cw_ref[j:j + 1, :] * cbuf[8 - npre + j:8 - npre + j + tb, :]
    y = _silu(y)

    @pl.when(t == nb - 1)
    def _():
        cnew_ref[0] = cbuf[8 + tb - npre:8 + tb, :]

    ba = ba_ref[...]
    beta = _sigmoid(ba)
    gg = -jnp.exp(av_ref[...]) * _softplus(ba + dv_ref[...])
    zz = z_ref[...]

    pad = C - tb if tb < C else 0
    rows = tb + pad

    def padr(a):
        if pad == 0:
            return a
        return jnp.concatenate([a, jnp.zeros((pad, a.shape[1]), a.dtype)], axis=0)

    def l2n(a, scale):
        parts = []
        for h in range(H):
            ah = a[:, h * LANES:(h + 1) * LANES]
            parts.append(ah * lax.rsqrt(jnp.sum(ah * ah, axis=-1, keepdims=True) + EPS) * scale)
        return jnp.concatenate(parts, axis=1)

    q_all = padr(l2n(y[:, 0:H * DN_DK], DN_DK ** -0.5))
    k_all = padr(l2n(y[:, H * DN_DK:2 * H * DN_DK], 1.0))
    v_all = padr(y[:, 2 * H * DN_DK:])
    beta = padr(beta)
    gg = padr(gg)

    ri = lax.broadcasted_iota(jnp.int32, (HC, HC), 0)
    ci = lax.broadcasted_iota(jnp.int32, (HC, HC), 1)
    same = (ri // C) == (ci // C)
    low = same & ((ri % C) >= (ci % C))
    strict = same & ((ri % C) > (ci % C))
    rowhead = lax.broadcasted_iota(jnp.int32, (HC, LANES), 0) // C
    tr = lax.broadcasted_iota(jnp.int32, (C, C), 0)
    tc = lax.broadcasted_iota(jnp.int32, (C, C), 1)
    tri = (tr >= tc).astype(F32)

    s_all = s_sc[...]
    outs = []
    for c in range(rows // C):
        r0 = c * C
        kst = _stack_heads(k_all[r0:r0 + C])
        qst = _stack_heads(q_all[r0:r0 + C])
        vst = _stack_heads(v_all[r0:r0 + C])
        gcm = _mm3(tri, gg[r0:r0 + C])
        bcol = _stack_cols(beta[r0:r0 + C], 0)
        gcol = _stack_cols(gcm, H)
        glast = jnp.concatenate(
            [jnp.broadcast_to(gcm[C - 1:C, H + h:H + h + 1], (C, 1)) for h in range(H)], axis=0)
        grow = jnp.transpose(jnp.broadcast_to(gcol, (HC, LANES)))[0:1, :]
        decay = jnp.where(low, jnp.exp(jnp.where(low, gcol - grow, 0.0)), 0.0)
        kb = kst * bcol
        kst16 = kst.astype(BF16)
        lmat = jnp.where(strict, _dot_nt(kb.astype(BF16), kst16) * decay, 0.0)
        attn = _dot_nt(qst.astype(BF16), kst16) * decay
        tinv = _unit_lower_inverse(lmat, ri, ci)
        egc = jnp.exp(gcol)
        sol = _mm3(tinv, jnp.concatenate([vst * bcol, kb * egc], axis=1))
        u = sol[:, :LANES]
        w = sol[:, LANES:]
        qd = qst * egc
        kd = kst * jnp.exp(glast - gcol)
        kdt = jnp.transpose(kd).astype(BF16)
        egl = jnp.concatenate(
            [jnp.broadcast_to(jnp.exp(gcm[C - 1:C, H + h:H + h + 1]), (1, LANES)) for h in range(H)],
            axis=1)
        r = _dot(jnp.concatenate([w, qd], axis=0).astype(BF16), s_all.astype(BF16))
        v_new = u - _diag_blocks(r[:HC], rowhead)
        o = _diag_blocks(r[HC:], rowhead) + _dot(attn.astype(BF16), v_new.astype(BF16))
        vbd = jnp.concatenate([jnp.where(rowhead == h, v_new, 0.0) for h in range(H)], axis=1)
        s_all = s_all * egl + _dot(kdt, vbd.astype(BF16))
        o = o * lax.rsqrt(jnp.mean(o * o, axis=-1, keepdims=True) + EPS) * ng_ref[...]
        outs.append(_unstack_heads(o, C))
    s_sc[...] = s_all
    o_all = outs[0] if len(outs) == 1 else jnp.concatenate(outs, axis=0)
    o_ref[...] = (o_all[:tb] * _silu(zz)).astype(o_ref.dtype)

    @pl.when(t == nb - 1)
    def _():
        for h in range(H):
            snew_ref[0, h] = s_all[:, h * DN_DV:(h + 1) * DN_DV]


def _deltanet(p, ba, prev, cw, avec, dvec, ng, s0, B, T, act_dtype):
    tb = min(T, 256)
    nb = T // tb
    return pl.pallas_call(
        functools.partial(_dn_kernel, tb=tb, nb=nb),
        grid=(B, nb),
        in_specs=[pl.BlockSpec((tb, DN_QKV), lambda b_, t: (b_ * nb + t, P_QKV // DN_QKV)),
                  pl.BlockSpec((tb, 512), lambda b_, t: (b_ * nb + t, P_Z // 512)),
                  pl.BlockSpec((tb, LANES), lambda b_, t: (b_ * nb + t, 0)),
                  pl.BlockSpec((1, DN_CONV - 1, DN_QKV), lambda b_, t: (b_, 0, 0)),
                  pl.BlockSpec((DN_CONV, DN_QKV), lambda b_, t: (0, 0)),
                  pl.BlockSpec((1, LANES), lambda b_, t: (0, 0)),
                  pl.BlockSpec((1, LANES), lambda b_, t: (0, 0)),
                  pl.BlockSpec((1, DN_DV), lambda b_, t: (0, 0)),
                  pl.BlockSpec((1, DN_HEADS, DN_DK, DN_DV), lambda b_, t: (b_, 0, 0, 0))],
        out_specs=[pl.BlockSpec((tb, DN_HEADS * DN_DV), lambda b_, t: (b_ * nb + t, 0)),
                   pl.BlockSpec((1, DN_CONV - 1, DN_QKV), lambda b_, t: (b_, 0, 0)),
                   pl.BlockSpec((1, DN_HEADS, DN_DK, DN_DV), lambda b_, t: (b_, 0, 0, 0))],
        out_shape=[SDS((B * T, DN_HEADS * DN_DV), act_dtype),
                   SDS((B, DN_CONV - 1, DN_QKV), F32),
                   SDS((B, DN_HEADS, DN_DK, DN_DV), F32)],
        scratch_shapes=[pltpu.VMEM((tb + 8, DN_QKV), F32),
                        pltpu.VMEM((DN_DK, DN_HEADS * DN_DV), F32)],
        compiler_params=_cparams("parallel", "arbitrary"),
        name="deltanet",
    )(p, p, ba, prev, cw, avec, dvec, ng, s0)


def _qknorm_kernel(q_ref, k_ref, v_ref, gq_ref, gk_ref, c_ref, s1_ref, s2_ref,
                   qo_ref, kf_ref, kb_ref, vb_ref):
    ri = lax.broadcasted_iota(jnp.int32, (LANES, LANES), 0)
    ci = lax.broadcasted_iota(jnp.int32, (LANES, LANES), 1)
    seg = ((ri // DIFF_DQK) == (ci // DIFF_DQK)).astype(BF16)
    cos = c_ref[...]
    s1 = s1_ref[...]
    s2 = s2_ref[...]

    def norm_rope(x, g):
        sq = x * x
        hi = sq.astype(BF16)
        r1 = sq - hi.astype(F32)
        mid = r1.astype(BF16)
        lo = (r1 - mid.astype(F32)).astype(BF16)
        tot = _dot(hi, seg) + (_dot(mid, seg) + _dot(lo, seg))
        y = x * lax.rsqrt(tot * (1.0 / DIFF_DQK) + EPS) * g
        return y * cos + pltpu.roll(y, LANES - ROT_DIM // 2, 1) * s1 + pltpu.roll(y, ROT_DIM // 2, 1) * s2

    for c in range(2 * DIFF_HEADS * DIFF_DQK // LANES):
        sl = slice(c * LANES, (c + 1) * LANES)
        qn = norm_rope(q_ref[:, sl], gq_ref[...])
        kn = norm_rope(k_ref[:, sl], gk_ref[...])
        qo_ref[:, sl] = (qn * (DIFF_DQK ** -0.5)).astype(qo_ref.dtype)
        kf_ref[:, sl] = kn
        kb_ref[:, sl] = kn.astype(kb_ref.dtype)
    vb_ref[...] = v_ref[...].astype(vb_ref.dtype)


def _qknorm(p, gq, gk, cos, s1, s2, n, T, act_dtype):
    tm = min(T, 512) if T % 16 == 0 else n
    nrep = T // tm if T % 16 == 0 else 1
    W = 1024
    tab = pl.BlockSpec((tm, LANES), lambda i: (i % nrep, 0))
    vec = pl.BlockSpec((1, LANES), lambda i: (0, 0))
    return pl.pallas_call(
        _qknorm_kernel,
        grid=(n // tm,),
        in_specs=[pl.BlockSpec((tm, W), lambda i: (i, P_DQ // W)),
                  pl.BlockSpec((tm, W), lambda i: (i, P_DK // W)),
                  pl.BlockSpec((tm, W), lambda i: (i, P_DV // W)),
                  vec, vec, tab, tab, tab],
        out_specs=[pl.BlockSpec((tm, W), lambda i: (i, 0))] * 4,
        out_shape=[SDS((n, W), act_dtype), SDS((n, W), F32), SDS((n, W), act_dtype), SDS((n, W), act_dtype)],
        compiler_params=_cparams("parallel"),
        name="qknorm_rope",
    )(p, p, p, gq, gk, cos, s1, s2)


def _rope_tables(pos):
    half = ROT_DIM // 2
    inv = ROPE_THETA ** (-jnp.arange(0, ROT_DIM, 2, dtype=F32) / ROT_DIM)
    ang = pos.astype(F32)[:, None] * inv[None, :]
    cos, sin = jnp.cos(ang), jnp.sin(ang)
    T = pos.shape[0]
    one = jnp.ones((T, DIFF_DQK - ROT_DIM), F32)
    zero = jnp.zeros((T, DIFF_DQK - ROT_DIM), F32)
    zh = jnp.zeros((T, half), F32)
    c64 = jnp.concatenate([cos, cos, one], axis=1)
    s1_64 = jnp.concatenate([-sin, zh, zero], axis=1)
    s2_64 = jnp.concatenate([zh, sin, zero], axis=1)
    return tuple(jnp.concatenate([a, a], axis=1) for a in (c64, s1_64, s2_64))


def _lambda(l1q, l1k, l2q, l2k, lam_init):
    a = jnp.sum(l1q * l1k, axis=-1, keepdims=True)
    b = jnp.sum(l2q * l2k, axis=-1, keepdims=True)
    return jnp.exp(a) - jnp.exp(b) + lam_init


def _diff_finish(acc0, l0, acc1, l1, lam, g, lam_init):
    o = acc0 / l0 - lam * (acc1 / l1)
    o = o * lax.rsqrt(jnp.mean(o * o, axis=-1, keepdims=True) + EPS) * g
    return o * (1.0 - lam_init)


def _flash_kernel(q_ref, k_ref, v_ref, l1q_ref, l1k_ref, l2q_ref, l2k_ref, g_ref, o_ref, *, tq, lam_init):
    qi = pl.program_id(2)
    q = q_ref[...]
    lane = lax.broadcasted_iota(jnp.int32, (1, LANES), 1)
    qs = [jnp.where((lane // DIFF_DQK) == j, q, jnp.zeros_like(q)) for j in range(2)]

    def block(off, carry, keep):
        k = k_ref[pl.ds(off, tq), :]
        v = v_ref[pl.ds(off, tq), :]
        ss = [_dot_nt(qs[j], k) for j in range(2)]
        out = []
        for j in range(2):
            m_prev, l_prev, a_prev = carry[j]
            s = ss[j] if keep is None else jnp.where(keep, ss[j], NEG)
            m_new = jnp.maximum(m_prev, jnp.max(s, axis=-1, keepdims=True))
            alpha = jnp.exp(m_prev - m_new)
            p = jnp.exp(s - m_new)
            l_new = alpha * l_prev + jnp.sum(p, axis=-1, keepdims=True)
            a_new = alpha * a_prev + _dot(p.astype(BF16), v)
            out.append((m_new, l_new, a_new))
        return tuple(out)

    one = (jnp.full((tq, 1), NEG, F32), jnp.zeros((tq, 1), F32), jnp.zeros((tq, DIFF_DV), F32))
    carry = lax.fori_loop(0, qi, lambda ki, c: block(pl.multiple_of(ki * tq, tq), c, None), (one, one))
    rr = lax.broadcasted_iota(jnp.int32, (tq, tq), 0)
    cc = lax.broadcasted_iota(jnp.int32, (tq, tq), 1)
    (m0, l0, a0), (m1, l1, a1) = block(pl.multiple_of(qi * tq, tq), carry, cc <= rr)
    del m0, m1
    lam = _lambda(l1q_ref[...], l1k_ref[...], l2q_ref[...], l2k_ref[...], lam_init)
    o_ref[...] = _diff_finish(a0, l0, a1, l1, lam, g_ref[...], lam_init).astype(o_ref.dtype)


def _flash_attention(q, k, v, lparams, g, B, T, lam_init):
    tq = min(T, 512)
    nq = T // tq
    vec64 = pl.BlockSpec((1, DIFF_DQK), lambda b_, h, qi: (0, 0))
    kv = pl.BlockSpec((T, LANES), lambda b_, h, qi: (b_, h))
    return pl.pallas_call(
        functools.partial(_flash_kernel, tq=tq, lam_init=lam_init),
        grid=(B, DIFF_HEADS, nq),
        in_specs=[pl.BlockSpec((tq, LANES), lambda b_, h, qi: (b_ * nq + qi, h)), kv, kv,
                  vec64, vec64, vec64, vec64,
                  pl.BlockSpec((1, DIFF_DV), lambda b_, h, qi: (0, 0))],
        out_specs=pl.BlockSpec((tq, LANES), lambda b_, h, qi: (b_ * nq + qi, h)),
        out_shape=SDS((B * T, DIFF_HEADS * DIFF_DV), BF16),
        compiler_params=_cparams("parallel", "parallel", "arbitrary"),
        name="flash_diff_attention",
    )(q, k, v, *lparams, g)


def _paged_kernel(pt_ref, q_ref, kn_ref, vn_ref, l1q_ref, l1k_ref, l2q_ref, l2k_ref, g_ref, e_ref, *rest,
                  pps, nsteps, tnew, lam_init):
    del pt_ref
    kv_refs = rest[:2 * pps]
    o_ref, m_sc, l_sc, acc_sc = rest[2 * pps:]
    s_id = pl.program_id(1)
    nsub = 2 * DIFF_HEADS
    nrow = nsub * tnew
    kdim = nsub * DIFF_DQK

    @pl.when(s_id == 0)
    def _():
        m_sc[...] = jnp.full(m_sc.shape, NEG, F32)
        l_sc[...] = jnp.zeros(l_sc.shape, F32)
        acc_sc[...] = jnp.zeros(acc_sc.shape, F32)

    ntok = kn_ref.shape[-1]
    er = lax.broadcasted_iota(jnp.int32, (nrow, ntok * DIFF_HEADS), 0)
    ec = lax.broadcasted_iota(jnp.int32, (nrow, ntok * DIFF_HEADS), 1)
    own_head = (ec % DIFF_HEADS) == (er // (2 * tnew))

    def update(k_refs, v_refs, keep):
        ss = [_dot(q_ref[0], k[...].reshape(kdim, ntok).astype(BF16)) for k in k_refs]
        if keep is not None:
            ss = [jnp.where(keep, s, NEG) for s in ss]
        m_prev = m_sc[...]
        m_new = m_prev
        for s in ss:
            m_new = jnp.maximum(m_new, jnp.max(s, axis=-1, keepdims=True))
        alpha = jnp.exp(m_prev - m_new)
        l_new = alpha * l_sc[...]
        acc = alpha * acc_sc[...]
        for s, v in zip(ss, v_refs):
            p = jnp.exp(s - m_new)
            if keep is not None:
                p = jnp.where(keep, p, 0.0)
            l_new = l_new + jnp.sum(p, axis=-1, keepdims=True)
            p_wide = _dot(p.astype(BF16), e_ref[...])
            p_bd = jnp.where(own_head, p_wide, 0.0).astype(BF16)
            acc = acc + _dot(p_bd, v[...].reshape(ntok * DIFF_HEADS, DIFF_DV).astype(BF16))
        l_sc[...] = l_new
        acc_sc[...] = acc
        m_sc[...] = m_new

    @pl.when(s_id < nsteps - 1)
    def _():
        update(kv_refs[:pps], kv_refs[pps:], None)

    @pl.when(s_id == nsteps - 1)
    def _():
        rr = lax.broadcasted_iota(jnp.int32, (nrow, ntok), 0)
        cc = lax.broadcasted_iota(jnp.int32, (nrow, ntok), 1)
        update([kn_ref.at[0]], [vn_ref.at[0]], cc <= (rr % tnew))
        lam = _lambda(l1q_ref[...], l1k_ref[...], l2q_ref[...], l2k_ref[...], lam_init)
        o = acc_sc[...] / l_sc[...]
        rows = 2 * tnew
        for hv in range(DIFF_HEADS):
            o1 = o[hv * rows:hv * rows + tnew]
            o2 = o[hv * rows + tnew:(hv + 1) * rows]
            d = o1 - lam * o2
            d = d * lax.rsqrt(jnp.mean(d * d, axis=-1, keepdims=True) + EPS) * g_ref[...]
            o_ref[0, hv * tnew:(hv + 1) * tnew, :] = d * (1.0 - lam_init)


def _paged_attention(q_bd, k_new_t, v_new, cache_kt, cache_v, page_table, layer, lparams, g, lam_init, tnew):
    DB, n_pages = page_table.shape
    page = cache_v.shape[2]
    pps = 8 if n_pages % 8 == 0 else 1
    nsteps = n_pages // pps + 1
    nsub = 2 * DIFF_HEADS
    nrow = nsub * tnew

    def kv_spec(r, shape):
        def imap(b_, s, pt):
            idx = jnp.minimum(s * pps + r, n_pages - 1)
            return (layer, pt[b_ * n_pages + idx], 0, 0, 0)
        return pl.BlockSpec((None, None) + shape, imap)

    vec64 = pl.BlockSpec((1, DIFF_DQK), lambda b_, s, pt: (0, 0))
    in_specs = [pl.BlockSpec((1, nrow, nsub * DIFF_DQK), lambda b_, s, pt: (b_, 0, 0)),
                pl.BlockSpec((1, nsub, DIFF_DQK, page), lambda b_, s, pt: (b_, 0, 0, 0)),
                pl.BlockSpec((1, page, DIFF_HEADS, DIFF_DV), lambda b_, s, pt: (b_, 0, 0, 0)),
                vec64, vec64, vec64, vec64,
                pl.BlockSpec((1, DIFF_DV), lambda b_, s, pt: (0, 0)),
                pl.BlockSpec((page, page * DIFF_HEADS), lambda b_, s, pt: (0, 0))]
    spread = (jnp.arange(page * DIFF_HEADS)[None, :] // DIFF_HEADS == jnp.arange(page)[:, None]).astype(BF16)
    in_specs += [kv_spec(r, (nsub, DIFF_DQK, page)) for r in range(pps)]
    in_specs += [kv_spec(r, (page, DIFF_HEADS, DIFF_DV)) for r in range(pps)]
    grid_spec = pltpu.PrefetchScalarGridSpec(
        num_scalar_prefetch=1,
        grid=(DB, nsteps),
        in_specs=in_specs,
        out_specs=pl.BlockSpec((1, DIFF_HEADS * tnew, DIFF_DV), lambda b_, s, pt: (b_, 0, 0)),
        scratch_shapes=[pltpu.VMEM((nrow, 1), F32), pltpu.VMEM((nrow, 1), F32),
                        pltpu.VMEM((nrow, DIFF_DV), F32)])
    return pl.pallas_call(
        functools.partial(_paged_kernel, pps=pps, nsteps=nsteps, tnew=tnew, lam_init=lam_init),
        grid_spec=grid_spec,
        out_shape=SDS((DB, DIFF_HEADS * tnew, DIFF_DV), F32),
        compiler_params=_cparams("parallel", "arbitrary"),
        name="paged_diff_attention",
    )(page_table.reshape(-1), q_bd, k_new_t, v_new, *lparams, g, spread,
      *([cache_kt] * pps), *([cache_v] * pps))


def _outproj_kernel(a_ref, b_ref, c_ref, wa_ref, wb_ref, wc_ref, x_ref, o_ref):
    acc = _dot(a_ref[...].astype(BF16), wa_ref[...])
    acc = acc + _dot(b_ref[...].astype(BF16), wb_ref[...])
    acc = acc + _dot(c_ref[...].astype(BF16), wc_ref[...])
    o_ref[...] = x_ref[...] + acc


def _out_proj(conf, odn, att, w_out, x):
    n = x.shape[0]
    tm = min(n, 1024)
    tn = 512
    return pl.pallas_call(
        _outproj_kernel,
        grid=(n // tm, D_MODEL // tn),
        in_specs=[pl.BlockSpec((tm, 512), lambda i, j: (i, 0)),
                  pl.BlockSpec((tm, 512), lambda i, j: (i, 0)),
                  pl.BlockSpec((tm, 1024), lambda i, j: (i, 0)),
                  pl.BlockSpec((512, tn), lambda i, j: (0, j)),
                  pl.BlockSpec((512, tn), lambda i, j: (1, j)),
                  pl.BlockSpec((1024, tn), lambda i, j: (1, j)),
                  pl.BlockSpec((tm, tn), lambda i, j: (i, j))],
        out_specs=pl.BlockSpec((tm, tn), lambda i, j: (i, j)),
        out_shape=SDS((n, D_MODEL), F32),
        compiler_params=_cparams("parallel", "arbitrary"),
        name="out_proj",
    )(conf, odn, att, w_out, w_out, w_out, x)


META_GROUP = N_EXPERTS
META_RANK = N_EXPERTS + 1
ROUTED_W = D_MODEL + LANES


def _router_kernel(x_ref, g_ref, wr_ref, br_ref, hg_ref, cnt_ref, carry):
    i = pl.program_id(0)

    @pl.when(i == 0)
    def _():
        carry[...] = jnp.zeros(carry.shape, F32)

    x = x_ref[...]
    tm = x.shape[0]
    y = x * lax.rsqrt(jnp.mean(x * x, axis=-1, keepdims=True) + EPS)
    h = y * g_ref[...]
    hg_ref[:, :D_MODEL] = h
    logits = _mm3(h, wr_ref[...]) + br_ref[...]
    lane = lax.broadcasted_iota(jnp.int32, logits.shape, 1)
    big = jnp.int32(1 << 20)
    is_g = (lane >= N_EXPERTS) & (lane < N_EXPERTS + N_GROUPS)
    gl = jnp.where(is_g, logits, NEG)
    gmax = jnp.max(gl, axis=-1, keepdims=True)
    gsel = jnp.min(jnp.where(is_g & (gl == gmax), lane, big), axis=-1, keepdims=True) - N_EXPERTS
    gw = 1.0 / jnp.sum(jnp.where(is_g, jnp.exp(gl - gmax), 0.0), axis=-1, keepdims=True)
    in_grp = (lane >= gsel * EXPERTS_PER_GROUP) & (lane < (gsel + 1) * EXPERTS_PER_GROUP)
    el = jnp.where(in_grp, logits, NEG)
    v1 = jnp.max(el, axis=-1, keepdims=True)
    i1 = jnp.min(jnp.where(in_grp & (el == v1), lane, big), axis=-1, keepdims=True)
    rest = in_grp & (lane != i1)
    el2 = jnp.where(rest, logits, NEG)
    v2 = jnp.max(el2, axis=-1, keepdims=True)
    i2 = jnp.min(jnp.where(rest & (el2 == v2), lane, big), axis=-1, keepdims=True)
    e2 = jnp.exp(v2 - v1)
    den = 1.0 + e2
    gates = jnp.where(lane == i1, (1.0 / den) * gw, 0.0) + jnp.where(lane == i2, (e2 / den) * gw, 0.0)
    onehot = jnp.where(lane == gsel, 1.0, 0.0)
    rr = lax.broadcasted_iota(jnp.int32, (tm, tm), 0)
    cc = lax.broadcasted_iota(jnp.int32, (tm, tm), 1)
    before = _dot(jnp.where(cc < rr, 1.0, 0.0).astype(BF16), onehot.astype(BF16)) + carry[...]
    rank = jnp.sum(onehot * before, axis=-1, keepdims=True)
    hg_ref[:, D_MODEL:] = (gates + jnp.where(lane == META_GROUP, gsel.astype(F32), 0.0)
                           + jnp.where(lane == META_RANK, rank, 0.0))
    carry[...] = carry[...] + jnp.sum(onehot, axis=0, keepdims=True)
    cnt_ref[...] = carry[...]


def _router(x, g, wr, br):
    n = x.shape[0]
    tm = min(n, 512)
    return pl.pallas_call(
        _router_kernel,
        grid=(n // tm,),
        in_specs=[pl.BlockSpec((tm, D_MODEL), lambda i: (i, 0)),
                  pl.BlockSpec((1, D_MODEL), lambda i: (0, 0)),
                  pl.BlockSpec((D_MODEL, LANES), lambda i: (0, 0)),
                  pl.BlockSpec((1, LANES), lambda i: (0, 0))],
        out_specs=[pl.BlockSpec((tm, ROUTED_W), lambda i: (i, 0)),
                   pl.BlockSpec((1, LANES), lambda i: (0, 0))],
        out_shape=[SDS((n, ROUTED_W), F32), SDS((1, LANES), F32)],
        scratch_shapes=[pltpu.VMEM((1, LANES), F32)],
        compiler_params=_cparams("arbitrary"),
        name="norm_router",
    )(x, g, wr, br)


def _route_plan(hg, counts, tile, n_tiles):
    cnt = counts[0, :N_GROUPS].astype(jnp.int32)
    size = ((cnt + tile - 1) // tile) * tile
    start = jnp.cumsum(size) - size
    grp = hg[:, D_MODEL + META_GROUP].astype(jnp.int32)
    rank = hg[:, D_MODEL + META_RANK].astype(jnp.int32)
    dest = rank
    for k in range(N_GROUPS):
        dest = dest + jnp.where(grp == k, start[k], 0)
    tile_start = jnp.arange(n_tiles, dtype=jnp.int32) * tile
    tile_group = jnp.sum((tile_start[:, None] >= start[None, :]).astype(jnp.int32), axis=1) - 1
    n_used = (jnp.sum(size) // tile).astype(jnp.int32).reshape(1)
    return dest, jnp.clip(tile_group, 0, N_GROUPS - 1), n_used


def _row_copy(src_ref, src_row, dst_ref, dst_row, sem):
    return pltpu.make_async_copy(src_ref.at[pl.ds(src_row, 1), :], dst_ref.at[pl.ds(dst_row, 1), :], sem)


def _dispatch_kernel(dest_ref, hg_ref, xs_in_ref, xs_ref, sem, *, rows):
    del xs_in_ref
    base = pl.program_id(0) * rows

    def issue(r, c):
        _row_copy(hg_ref, r, xs_ref, dest_ref[base + r], sem).start()
        return c

    def drain(r, c):
        _row_copy(hg_ref, 0, xs_ref, 0, sem).wait()
        return c

    lax.fori_loop(0, rows, issue, 0, unroll=8)
    lax.fori_loop(0, rows, drain, 0, unroll=8)


def _dispatch(dest, hg, n_rows):
    n = hg.shape[0]
    rows = min(n, 256)
    grid_spec = pltpu.PrefetchScalarGridSpec(
        num_scalar_prefetch=1, grid=(n // rows,),
        in_specs=[pl.BlockSpec((rows, ROUTED_W), lambda i, d: (i, 0)),
                  pl.BlockSpec(memory_space=pl.ANY)],
        out_specs=pl.BlockSpec(memory_space=pl.ANY),
        scratch_shapes=[pltpu.SemaphoreType.DMA(())])
    return pl.pallas_call(
        functools.partial(_dispatch_kernel, rows=rows),
        grid_spec=grid_spec,
        out_shape=SDS((n_rows, ROUTED_W), F32),
        input_output_aliases={2: 0},
        compiler_params=_cparams("arbitrary"),
        name="moe_dispatch",
    )(dest, hg, jnp.zeros((n_rows, ROUTED_W), F32))


def _experts_kernel(tg_ref, nu_ref, xs_ref, wg_ref, wu_ref, wd_ref, o_ref):
    i = pl.program_id(0)
    e = pl.program_id(1)

    @pl.when(e == 0)
    def _():
        o_ref[...] = jnp.zeros(o_ref.shape, F32)

    @pl.when(i < nu_ref[0])
    def _():
        h = xs_ref[:, :D_MODEL].astype(BF16)
        meta = xs_ref[:, D_MODEL:]
        lane = lax.broadcasted_iota(jnp.int32, meta.shape, 1)
        col = tg_ref[i] * EXPERTS_PER_GROUP + e
        ge = jnp.sum(jnp.where(lane == col, meta, 0.0), axis=-1, keepdims=True)
        act = _silu(_dot(h, wg_ref[...])) * _dot(h, wu_ref[...]) * ge
        o_ref[...] += _dot(act.astype(BF16), wd_ref[...])


def _experts(tile_group, n_used, xs, wg, wu, wd, tile):
    n_rows = xs.shape[0]
    wspec = lambda shape: pl.BlockSpec((None,) + shape, lambda i, e, tg, nu: (tg[i] * EXPERTS_PER_GROUP + e, 0, 0))
    grid_spec = pltpu.PrefetchScalarGridSpec(
        num_scalar_prefetch=2, grid=(n_rows // tile, EXPERTS_PER_GROUP),
        in_specs=[pl.BlockSpec((tile, ROUTED_W), lambda i, e, tg, nu: (i, 0)),
                  wspec((D_MODEL, D_EXPERT)), wspec((D_MODEL, D_EXPERT)), wspec((D_EXPERT, D_MODEL))],
        out_specs=pl.BlockSpec((tile, D_MODEL), lambda i, e, tg, nu: (i, 0)))
    return pl.pallas_call(
        _experts_kernel,
        grid_spec=grid_spec,
        out_shape=SDS((n_rows, D_MODEL), F32),
        compiler_params=_cparams("parallel", "arbitrary"),
        name="moe_experts",
    )(tile_group, n_used, xs, wg, wu, wd)


def _combine_kernel(dest_ref, x_ref, ys_ref, o_ref, buf, sem, *, rows):
    base = pl.program_id(0) * rows

    def issue(r, c):
        _row_copy(ys_ref, dest_ref[base + r], buf, r, sem).start()
        return c

    def drain(r, c):
        _row_copy(ys_ref, 0, buf, 0, sem).wait()
        return c

    lax.fori_loop(0, rows, issue, 0, unroll=8)
    lax.fori_loop(0, rows, drain, 0, unroll=8)
    o_ref[...] = x_ref[...] + buf[...]


def _combine(dest, x, ys):
    n = x.shape[0]
    rows = min(n, 256)
    grid_spec = pltpu.PrefetchScalarGridSpec(
        num_scalar_prefetch=1, grid=(n // rows,),
        in_specs=[pl.BlockSpec((rows, D_MODEL), lambda i, d: (i, 0)),
                  pl.BlockSpec(memory_space=pl.ANY)],
        out_specs=pl.BlockSpec((rows, D_MODEL), lambda i, d: (i, 0)),
        scratch_shapes=[pltpu.VMEM((rows, D_MODEL), F32), pltpu.SemaphoreType.DMA(())])
    return pl.pallas_call(
        functools.partial(_combine_kernel, rows=rows),
        grid_spec=grid_spec,
        out_shape=SDS((n, D_MODEL), F32),
        compiler_params=_cparams("arbitrary"),
        name="moe_combine",
    )(dest, x, ys)


def _moe_block(x, g, wr, br, wg, wu, wd):
    n = x.shape[0]
    tile = min(n, 512)
    n_tiles = n // tile + N_GROUPS
    hg, counts = _router(x, g, wr, br)
    dest, tile_group, n_used = _route_plan(hg, counts, tile, n_tiles)
    xs = _dispatch(dest, hg, n_tiles * tile)
    ys = _experts(tile_group, n_used, xs, wg, wu, wd, tile)
    return _combine(dest, x, ys)


def _row(v, width=None):
    v = v.astype(F32).reshape(1, -1)
    if width is not None and v.shape[1] < width:
        v = jnp.pad(v, ((0, 0), (0, width - v.shape[1])))
    return v


def _prep_layer(l, w):
    wi = w['w_in'][l]
    w_main = jnp.concatenate([wi[:, 1024:2560], wi[:, 2560:3072], wi[:, 0:1024], wi[:, 3080:6152]],
                             axis=1).astype(BF16)
    w_ba = jnp.pad(wi[:, 3072:3080], ((0, 0), (0, LANES - 8))).astype(BF16)
    wr = jnp.concatenate([w['router_expert_w'][l], w['router_group_w'][l]], axis=1)
    wr = jnp.pad(wr, ((0, 0), (0, LANES - wr.shape[1])))
    br = _row(jnp.concatenate([w['router_expert_b'][l], w['router_group_b'][l]]), LANES)
    lane_pad = lambda v: _row(jnp.concatenate([jnp.zeros((DN_HEADS,), F32), v.astype(F32)]), LANES)
    return dict(
        norm1_g=_row(w['norm1_g'][l]), w_main=w_main, w_ba=w_ba,
        conf_w=w['conf_dw_w'][l], conf_b=_row(w['conf_dw_b'][l]),
        conf_g=_row(w['conf_ln_g'][l]), conf_bb=_row(w['conf_ln_b'][l]),
        dn_cw=w['dn_conv_w'][l], dn_av=lane_pad(w['dn_a_log'][l]), dn_dv=lane_pad(w['dn_dt_bias'][l]),
        dn_ng=_row(w['dn_norm_g'][l]),
        gq=_row(jnp.tile(w['diff_q_norm_g'][l], 2)), gk=_row(jnp.tile(w['diff_k_norm_g'][l], 2)),
        lparams=tuple(_row(w[k][l]) for k in ('diff_lambda_q1', 'diff_lambda_k1',
                                               'diff_lambda_q2', 'diff_lambda_k2')),
        subln=_row(w['diff_subln_g'][l]),
        w_out=w['w_out'][l].astype(BF16), norm2_g=_row(w['norm2_g'][l]), wr=wr, br=br,
        wg=w['moe_w_gate'][l].astype(BF16), wu=w['moe_w_up'][l].astype(BF16),
        wd=w['moe_w_down'][l].astype(BF16))


def _layer(x, B, T, tables, conf_prev, dnc_prev, dn_s0, lp, lam_init, paged):
    n = B * T
    act_dtype = BF16 if T % 16 == 0 else F32
    p, ba = _in_proj(x, lp['norm1_g'], lp['w_main'], lp['w_ba'])
    conf, conf_new = _conformer(p, conf_prev, lp['conf_w'], lp['conf_b'], lp['conf_g'], lp['conf_bb'],
                                B, T, act_dtype)
    odn, dnc_new, dn_state = _deltanet(p, ba, dnc_prev, lp['dn_cw'], lp['dn_av'], lp['dn_dv'],
                                       lp['dn_ng'], dn_s0, B, T, act_dtype)
    qn, kf, kb, vb = _qknorm(p, lp['gq'], lp['gk'], *tables, n, T, act_dtype)
    v_f32 = p[:, P_DV:P_DV + DIFF_HEADS * DIFF_DV]
    k_out = kf.reshape(B, T, 2 * DIFF_HEADS, DIFF_DQK)
    v_out = v_f32.reshape(B, T, DIFF_HEADS, DIFF_DV)
    if paged is None:
        att = _flash_attention(qn, kb, vb, lp['lparams'], lp['subln'], B, T, lam_init)
    else:
        cache_kt, cache_v, page_table, layer = paged
        page = cache_v.shape[2]
        nsub = 2 * DIFF_HEADS
        q4 = qn.reshape(B, T, nsub, DIFF_DQK).transpose(0, 2, 1, 3)
        eye = jnp.eye(nsub, dtype=qn.dtype)
        q_bd = (q4[:, :, :, None, :] * eye[None, :, None, :, None]).reshape(B, nsub * T, nsub * DIFF_DQK)
        k_new_t = jnp.pad(k_out.transpose(0, 2, 3, 1), ((0, 0), (0, 0), (0, 0), (0, page - T)))
        v_new = jnp.pad(v_out, ((0, 0), (0, page - T), (0, 0), (0, 0)))
        att = _paged_attention(q_bd.astype(BF16), k_new_t, v_new, cache_kt, cache_v, page_table, layer,
                               lp['lparams'], lp['subln'], lam_init, T)
        att = att.reshape(B, DIFF_HEADS, T, DIFF_DV).transpose(0, 2, 1, 3).reshape(n, DIFF_HEADS * DIFF_DV)
    x2 = _out_proj(conf, odn, att, lp['w_out'], x)
    x3 = _moe_block(x2, lp['norm2_g'], lp['wr'], lp['br'], lp['wg'], lp['wu'], lp['wd'])
    return x3, (k_out, v_out, dn_state, dnc_new, conf_new)


def kernel(x_prompt, x_sample, cache_k, cache_v, page_table, state_delta, state_delta_conv, state_conf_conv, norm1_g, w_in, conf_dw_w, conf_dw_b, conf_ln_g, conf_ln_b, dn_conv_w, dn_a_log, dn_dt_bias, dn_norm_g, diff_q_norm_g, diff_k_norm_g, diff_lambda_q1, diff_lambda_k1, diff_lambda_q2, diff_lambda_k2, diff_subln_g, w_out, norm2_g, router_group_w, router_group_b, router_expert_w, router_expert_b, moe_w_gate, moe_w_up, moe_w_down):
    w = dict(norm1_g=norm1_g, w_in=w_in, conf_dw_w=conf_dw_w, conf_dw_b=conf_dw_b, conf_ln_g=conf_ln_g,
             conf_ln_b=conf_ln_b, dn_conv_w=dn_conv_w, dn_a_log=dn_a_log, dn_dt_bias=dn_dt_bias,
             dn_norm_g=dn_norm_g, diff_q_norm_g=diff_q_norm_g, diff_k_norm_g=diff_k_norm_g,
             diff_lambda_q1=diff_lambda_q1, diff_lambda_k1=diff_lambda_k1, diff_lambda_q2=diff_lambda_q2,
             diff_lambda_k2=diff_lambda_k2, diff_subln_g=diff_subln_g, w_out=w_out, norm2_g=norm2_g,
             router_group_w=router_group_w, router_group_b=router_group_b,
             router_expert_w=router_expert_w, router_expert_b=router_expert_b,
             moe_w_gate=moe_w_gate, moe_w_up=moe_w_up, moe_w_down=moe_w_down)
    B, S, _ = x_prompt.shape
    DB, T, _ = x_sample.shape
    depth = w_in.shape[0]
    past_len = page_table.shape[1] * cache_k.shape[2]
    cache_kt = jnp.transpose(cache_k, (0, 1, 3, 4, 2))
    tab_p = _rope_tables(jnp.arange(S, dtype=jnp.int32))
    tab_s = tuple(jnp.tile(a, (DB, 1)) for a in _rope_tables(past_len + jnp.arange(T, dtype=jnp.int32)))
    hp = x_prompt.reshape(B * S, D_MODEL)
    hs = x_sample.reshape(DB * T, D_MODEL)
    zero_conf = jnp.zeros((B, CONF_K - 1, CONF_W), F32)
    zero_dnc = jnp.zeros((B, DN_CONV - 1, DN_QKV), F32)
    zero_s = jnp.zeros((B, DN_HEADS, DN_DK, DN_DV), F32)
    outs_p, outs_s = [], []
    for l in range(depth):
        lp = _prep_layer(l, w)
        lam_init = 0.8 - 0.6 * math.exp(-0.3 * l)
        hp, sp = _layer(hp, B, S, tab_p, zero_conf, zero_dnc, zero_s, lp, lam_init, None)
        hs, ss = _layer(hs, DB, T, tab_s, state_conf_conv[l], state_delta_conv[l], state_delta[l], lp,
                        lam_init, (cache_kt, cache_v, page_table, l))
        outs_p.append(sp)
        outs_s.append(ss)
    st = lambda outs, i: jnp.stack([o[i] for o in outs])
    return (hp.reshape(B, S, D_MODEL), hs.reshape(DB, T, D_MODEL),
            st(outs_p, 0), st(outs_p, 1), st(outs_s, 0), st(outs_s, 1),
            st(outs_p, 2), st(outs_s, 2), st(outs_p, 3), st(outs_s, 3),
            st(outs_p, 4), st(outs_s, 4))
```

```python
import functools
import math

import jax
import jax.numpy as jnp
from jax import lax
from jax.experimental import pallas as pl
from jax.experimental.pallas import tpu as pltpu

F32 = jnp.float32
BF16 = jnp.bfloat16
SDS = jax.ShapeDtypeStruct

D_MODEL = 2048
CONF_W = 512
CONF_K = 31
DN_HEADS = 4
DN_DK = 128
DN_DV = 128
DN_CONV = 4
DN_CHUNK = 64
DN_QKV = DN_HEADS * (2 * DN_DK + DN_DV)
DIFF_HEADS = 8
DIFF_DQK = 64
DIFF_DV = 128
ROT_DIM = 16
ROPE_THETA = 500000.0
N_GROUPS = 4
EXPERTS_PER_GROUP = 4
N_EXPERTS = 16
D_EXPERT = 512
EPS = 1e-6
LANES = 128
VMEM_LIMIT = 48 * 1024 * 1024

P_QKV, P_Z, P_GLU, P_DQ, P_DK, P_DV = 0, 1536, 2048, 3072, 4096, 5120
P_W = 6144
NEG = -1e30


def _cparams(*sem):
    return pltpu.CompilerParams(dimension_semantics=sem, vmem_limit_bytes=VMEM_LIMIT)


def _dot(a, b):
    return jnp.dot(a, b, preferred_element_type=F32)


def _dot_nt(a, b):
    return lax.dot_general(a, b, (((1,), (1,)), ((), ())), preferred_element_type=F32)


def _split(a):
    hi = a.astype(BF16)
    return hi, (a - hi.astype(F32)).astype(BF16)


def _mm3(a, b):
    ah, al = _split(a)
    bh, bl = _split(b)
    return _dot(ah, bh) + (_dot(ah, bl) + _dot(al, bh))


def _mm3_nt(a, b):
    ah, al = _split(a)
    bh, bl = _split(b)
    return _dot_nt(ah, bh) + (_dot_nt(ah, bl) + _dot_nt(al, bh))


def _sigmoid(x):
    return 1.0 / (1.0 + jnp.exp(-x))


def _silu(x):
    return x * _sigmoid(x)


def _softplus(x):
    return jnp.maximum(x, 0.0) + jnp.log1p(jnp.exp(-jnp.abs(x)))


def _inproj_kernel(x_ref, g_ref, w_ref, wba_ref, o_ref, oba_ref, h_ref):
    @pl.when(pl.program_id(1) == 0)
    def _():
        x = x_ref[...]
        y = x * lax.rsqrt(jnp.mean(x * x, axis=-1, keepdims=True) + EPS)
        h_ref[...] = (y * g_ref[...]).astype(BF16)
        oba_ref[...] = _dot(h_ref[...], wba_ref[...])

    o_ref[...] = _dot(h_ref[...], w_ref[...])


def _in_proj(x, g, w_main, w_ba):
    n = x.shape[0]
    tm = min(n, 1024)
    tn = 1024
    return pl.pallas_call(
        _inproj_kernel,
        grid=(n // tm, P_W // tn),
        in_specs=[pl.BlockSpec((tm, D_MODEL), lambda i, j: (i, 0)),
                  pl.BlockSpec((1, D_MODEL), lambda i, j: (0, 0)),
                  pl.BlockSpec((D_MODEL, tn), lambda i, j: (0, j)),
                  pl.BlockSpec((D_MODEL, LANES), lambda i, j: (0, 0))],
        out_specs=[pl.BlockSpec((tm, tn), lambda i, j: (i, j)),
                   pl.BlockSpec((tm, LANES), lambda i, j: (i, 0))],
        out_shape=[SDS((n, P_W), F32), SDS((n, LANES), F32)],
        scratch_shapes=[pltpu.VMEM((tm, D_MODEL), BF16)],
        compiler_params=_cparams("parallel", "arbitrary"),
        name="in_proj",
    )(x, g, w_main, w_ba)


def _conf_kernel(p_ref, prev_ref, w_ref, b_ref, g_ref, bb_ref, o_ref, new_ref, ubuf, *, tt, nt):
    t = pl.program_id(1)
    hist = 32

    @pl.when(t == 0)
    def _():
        ubuf[0:2, :] = jnp.zeros((2, CONF_W), F32)
        ubuf[2:hist, :] = prev_ref[0]

    if nt > 1:
        @pl.when(t > 0)
        def _():
            ubuf[0:hist, :] = ubuf[tt:tt + hist, :]

    x = p_ref[...]
    ubuf[hist:hist + tt, :] = x[:, :CONF_W] * _sigmoid(x[:, CONF_W:])
    rc = min(tt, 64)
    for r0 in range(0, tt, rc):
        acc = jnp.zeros((rc, CONF_W), F32)
        for j in range(CONF_K):
            s = r0 + hist - (CONF_K - 1) + j
            acc = acc + w_ref[j:j + 1, :] * ubuf[s:s + rc, :]
        c = acc + b_ref[...]
        xc = c - jnp.mean(c, axis=-1, keepdims=True)
        y = xc * lax.rsqrt(jnp.mean(xc * xc, axis=-1, keepdims=True) + EPS)
        y = y * g_ref[...] + bb_ref[...]
        o_ref[r0:r0 + rc, :] = _silu(y).astype(o_ref.dtype)

    @pl.when(t == nt - 1)
    def _():
        new_ref[0] = ubuf[tt + hist - (CONF_K - 1):tt + hist, :]


def _conformer(p, prev, w, b, g, bb, B, T, act_dtype):
    tt = min(T, 256)
    nt = T // tt
    return pl.pallas_call(
        functools.partial(_conf_kernel, tt=tt, nt=nt),
        grid=(B, nt),
        in_specs=[pl.BlockSpec((tt, 2 * CONF_W), lambda b_, t: (b_ * nt + t, P_GLU // (2 * CONF_W))),
                  pl.BlockSpec((1, CONF_K - 1, CONF_W), lambda b_, t: (b_, 0, 0)),
                  pl.BlockSpec((CONF_K, CONF_W), lambda b_, t: (0, 0)),
                  pl.BlockSpec((1, CONF_W), lambda b_, t: (0, 0)),
                  pl.BlockSpec((1, CONF_W), lambda b_, t: (0, 0)),
                  pl.BlockSpec((1, CONF_W), lambda b_, t: (0, 0))],
        out_specs=[pl.BlockSpec((tt, CONF_W), lambda b_, t: (b_ * nt + t, 0)),
                   pl.BlockSpec((1, CONF_K - 1, CONF_W), lambda b_, t: (b_, 0, 0))],
        out_shape=[SDS((B * T, CONF_W), act_dtype), SDS((B, CONF_K - 1, CONF_W), F32)],
        scratch_shapes=[pltpu.VMEM((tt + 32, CONF_W), F32)],
        compiler_params=_cparams("parallel", "arbitrary"),
        name="conformer",
    )(p, prev, w, b, g, bb)


def _stack_heads(a):
    return jnp.concatenate([a[:, h * LANES:(h + 1) * LANES] for h in range(DN_HEADS)], axis=0)


def _unstack_heads(a, c):
    return jnp.concatenate([a[h * c:(h + 1) * c, :] for h in range(DN_HEADS)], axis=1)


def _stack_cols(a, lane0):
    return jnp.concatenate([a[:, lane0 + h:lane0 + h + 1] for h in range(DN_HEADS)], axis=0)


def _diag_blocks(x, rowhead):
    out = jnp.where(rowhead == 0, x[:, 0:LANES], 0.0)
    for h in range(1, DN_HEADS):
        out = out + jnp.where(rowhead == h, x[:, h * LANES:(h + 1) * LANES], 0.0)
    return out


def _unit_lower_inverse(lmat, ri, ci, mm1):
    eye = (ri == ci).astype(F32)
    d = jnp.where((ri // 16) == (ci // 16), lmat, 0.0)
    d2 = mm1(d, d)
    d4 = mm1(d2, d2)
    d8 = mm1(d4, d4)
    t = mm1(mm1(mm1(eye - d, eye + d2), eye + d4), eye + d8)
    m32 = jnp.where(((ri // 32) == (ci // 32)) & ((ri // 16) != (ci // 16)), lmat, 0.0)
    t = t - mm1(mm1(t, m32), t)
    m64 = jnp.where((ri // 32) != (ci // 32), lmat, 0.0)
    t = t - mm1(mm1(t, m64), t)
    resid = eye - t - _mm3(lmat, t)
    return t + mm1(t, resid)


def _dn_kernel(qkv_ref, z_ref, ba_ref, prev_ref, cw_ref, av_ref, dv_ref, ng_ref, s0_ref,
               o_ref, cnew_ref, snew_ref, cbuf, s_sc, *, tb, nb):
    t = pl.program_id(1)
    C = DN_CHUNK
    H = DN_HEADS
    HC = H * C
    npre = DN_CONV - 1
    if tb < C:
        mm, mm_nt = _mm3, _mm3_nt
    else:
        mm = lambda a, b: _dot(a.astype(BF16), b.astype(BF16))
        mm_nt = lambda a, b: _dot_nt(a.astype(BF16), b.astype(BF16))

    @pl.when(t == 0)
    def _():
        cbuf[0:8 - npre, :] = jnp.zeros((8 - npre, DN_QKV), F32)
        cbuf[8 - npre:8, :] = prev_ref[0]
        s_sc[...] = jnp.concatenate([s0_ref[0, h] for h in range(H)], axis=1)

    if nb > 1:
        @pl.when(t > 0)
        def _():
            cbuf[0:8, :] = cbuf[tb:tb + 8, :]

    cbuf[8:8 + tb, :] = qkv_ref[...]
    y = jnp.zeros((tb, DN_QKV), F32)
    for j in range(DN_CONV):
        y = y + cw_ref[j:j + 1, :] * cbuf[8 - npre + j:8 - npre + j + tb, :]
    y = _silu(y)

    @pl.when(t == nb - 1)
    def _():
        cnew_ref[0] = cbuf[8 + tb - npre:8 + tb, :]

    ba = ba_ref[...]
    beta = _sigmoid(ba)
    gg = -jnp.exp(av_ref[...]) * _softplus(ba + dv_ref[...])
    zz = z_ref[...]

    pad = C - tb if tb < C else 0
    rows = tb + pad

    def padr(a):
        if pad == 0:
            return a
        return jnp.concatenate([a, jnp.zeros((pad, a.shape[1]), a.dtype)], axis=0)

    def l2n(a, scale):
        parts = []
        for h in range(H):
            ah = a[:, h * LANES:(h + 1) * LANES]
            parts.append(ah * lax.rsqrt(jnp.sum(ah * ah, axis=-1, keepdims=True) + EPS) * scale)
        return jnp.concatenate(parts, axis=1)

    q_all = padr(l2n(y[:, 0:H * DN_DK], DN_DK ** -0.5))
    k_all = padr(l2n(y[:, H * DN_DK:2 * H * DN_DK], 1.0))
    v_all = padr(y[:, 2 * H * DN_DK:])
    beta = padr(beta)
    gg = padr(gg)

    ri = lax.broadcasted_iota(jnp.int32, (HC, HC), 0)
    ci = lax.broadcasted_iota(jnp.int32, (HC, HC), 1)
    same = (ri // C) == (ci // C)
    low = same & ((ri % C) >= (ci % C))
    strict = same & ((ri % C) > (ci % C))
    rowhead = lax.broadcasted_iota(jnp.int32, (HC, LANES), 0) // C
    tr = lax.broadcasted_iota(jnp.int32, (C, C), 0)
    tc = lax.broadcasted_iota(jnp.int32, (C, C), 1)
    tri = (tr >= tc).astype(F32)

    s_all = s_sc[...]
    outs = []
    for c in range(rows // C):
        r0 = c * C
        kst = _stack_heads(k_all[r0:r0 + C])
        qst = _stack_heads(q_all[r0:r0 + C])
        vst = _stack_heads(v_all[r0:r0 + C])
        gcm = _mm3(tri, gg[r0:r0 + C])
        bcol = _stack_cols(beta[r0:r0 + C], 0)
        gcol = _stack_cols(gcm, H)
        glast = jnp.concatenate(
            [jnp.broadcast_to(gcm[C - 1:C, H + h:H + h + 1], (C, 1)) for h in range(H)], axis=0)
        grow = jnp.transpose(jnp.broadcast_to(gcol, (HC, LANES)))[0:1, :]
        decay = jnp.where(low, jnp.exp(jnp.where(low, gcol - grow, 0.0)), 0.0)
        kb = kst * bcol
        lmat = jnp.where(strict, mm_nt(kb, kst) * decay, 0.0)
        attn = mm_nt(qst, kst) * decay
        tinv = _unit_lower_inverse(lmat, ri, ci, mm)
        egc = jnp.exp(gcol)
        sol = _mm3(tinv, jnp.concatenate([vst * bcol, kb * egc], axis=1))
        u = sol[:, :LANES]
        w = sol[:, LANES:]
        qd = qst * egc
        kd = kst * jnp.exp(glast - gcol)
        kdt = jnp.transpose(kd)
        egl = jnp.concatenate(
            [jnp.broadcast_to(jnp.exp(gcm[C - 1:C, H + h:H + h + 1]), (1, LANES)) for h in range(H)],
            axis=1)
        r = mm(jnp.concatenate([w, qd], axis=0), s_all)
        v_new = u - _diag_blocks(r[:HC], rowhead)
        o = _diag_blocks(r[HC:], rowhead) + mm(attn, v_new)
        vbd = jnp.concatenate([jnp.where(rowhead == h, v_new, 0.0) for h in range(H)], axis=1)
        s_all = s_all * egl + mm(kdt, vbd)
        o = o * lax.rsqrt(jnp.mean(o * o, axis=-1, keepdims=True) + EPS) * ng_ref[...]
        outs.append(_unstack_heads(o, C))
    s_sc[...] = s_all
    o_all = outs[0] if len(outs) == 1 else jnp.concatenate(outs, axis=0)
    o_ref[...] = (o_all[:tb] * _silu(zz)).astype(o_ref.dtype)

    @pl.when(t == nb - 1)
    def _():
        for h in range(H):
            snew_ref[0, h] = s_all[:, h * DN_DV:(h + 1) * DN_DV]


def _deltanet(p, ba, prev, cw, avec, dvec, ng, s0, B, T, act_dtype):
    tb = min(T, 256)
    nb = T // tb
    return pl.pallas_call(
        functools.partial(_dn_kernel, tb=tb, nb=nb),
        grid=(B, nb),
        in_specs=[pl.BlockSpec((tb, DN_QKV), lambda b_, t: (b_ * nb + t, P_QKV // DN_QKV)),
                  pl.BlockSpec((tb, 512), lambda b_, t: (b_ * nb + t, P_Z // 512)),
                  pl.BlockSpec((tb, LANES), lambda b_, t: (b_ * nb + t, 0)),
                  pl.BlockSpec((1, DN_CONV - 1, DN_QKV), lambda b_, t: (b_, 0, 0)),
                  pl.BlockSpec((DN_CONV, DN_QKV), lambda b_, t: (0, 0)),
                  pl.BlockSpec((1, LANES), lambda b_, t: (0, 0)),
                  pl.BlockSpec((1, LANES), lambda b_, t: (0, 0)),
                  pl.BlockSpec((1, DN_DV), lambda b_, t: (0, 0)),
                  pl.BlockSpec((1, DN_HEADS, DN_DK, DN_DV), lambda b_, t: (b_, 0, 0, 0))],
        out_specs=[pl.BlockSpec((tb, DN_HEADS * DN_DV), lambda b_, t: (b_ * nb + t, 0)),
                   pl.BlockSpec((1, DN_CONV - 1, DN_QKV), lambda b_, t: (b_, 0, 0)),
                   pl.BlockSpec((1, DN_HEADS, DN_DK, DN_DV), lambda b_, t: (b_, 0, 0, 0))],
        out_shape=[SDS((B * T, DN_HEADS * DN_DV), act_dtype),
                   SDS((B, DN_CONV - 1, DN_QKV), F32),
                   SDS((B, DN_HEADS, DN_DK, DN_DV), F32)],
        scratch_shapes=[pltpu.VMEM((tb + 8, DN_QKV), F32),
                        pltpu.VMEM((DN_DK, DN_HEADS * DN_DV), F32)],
        compiler_params=_cparams("parallel", "arbitrary"),
        name="deltanet",
    )(p, p, ba, prev, cw, avec, dvec, ng, s0)


def _qknorm_kernel(q_ref, k_ref, v_ref, gq_ref, gk_ref, c_ref, s1_ref, s2_ref,
                   qo_ref, kf_ref, kb_ref, vb_ref):
    ri = lax.broadcasted_iota(jnp.int32, (LANES, LANES), 0)
    ci = lax.broadcasted_iota(jnp.int32, (LANES, LANES), 1)
    seg = ((ri // DIFF_DQK) == (ci // DIFF_DQK)).astype(BF16)
    cos = c_ref[...]
    s1 = s1_ref[...]
    s2 = s2_ref[...]

    def norm_rope(x, g):
        sq = x * x
        hi = sq.astype(BF16)
        r1 = sq - hi.astype(F32)
        mid = r1.astype(BF16)
        lo = (r1 - mid.astype(F32)).astype(BF16)
        tot = _dot(hi, seg) + (_dot(mid, seg) + _dot(lo, seg))
        y = x * lax.rsqrt(tot * (1.0 / DIFF_DQK) + EPS) * g
        return y * cos + pltpu.roll(y, LANES - ROT_DIM // 2, 1) * s1 + pltpu.roll(y, ROT_DIM // 2, 1) * s2

    for c in range(2 * DIFF_HEADS * DIFF_DQK // LANES):
        sl = slice(c * LANES, (c + 1) * LANES)
        qn = norm_rope(q_ref[:, sl], gq_ref[...])
        kn = norm_rope(k_ref[:, sl], gk_ref[...])
        qo_ref[:, sl] = (qn * (DIFF_DQK ** -0.5)).astype(qo_ref.dtype)
        kf_ref[:, sl] = kn
        kb_ref[:, sl] = kn.astype(kb_ref.dtype)
    vb_ref[...] = v_ref[...].astype(vb_ref.dtype)


def _qknorm(p, gq, gk, cos, s1, s2, n, T, act_dtype):
    tm = min(T, 512) if T % 16 == 0 else n
    nrep = T // tm if T % 16 == 0 else 1
    W = 1024
    tab = pl.BlockSpec((tm, LANES), lambda i: (i % nrep, 0))
    vec = pl.BlockSpec((1, LANES), lambda i: (0, 0))
    return pl.pallas_call(
        _qknorm_kernel,
        grid=(n // tm,),
        in_specs=[pl.BlockSpec((tm, W), lambda i: (i, P_DQ // W)),
                  pl.BlockSpec((tm, W), lambda i: (i, P_DK // W)),
                  pl.BlockSpec((tm, W), lambda i: (i, P_DV // W)),
                  vec, vec, tab, tab, tab],
        out_specs=[pl.BlockSpec((tm, W), lambda i: (i, 0))] * 4,
        out_shape=[SDS((n, W), act_dtype), SDS((n, W), F32), SDS((n, W), act_dtype), SDS((n, W), act_dtype)],
        compiler_params=_cparams("parallel"),
        name="qknorm_rope",
    )(p, p, p, gq, gk, cos, s1, s2)


def _rope_tables(pos):
    half = ROT_DIM // 2
    inv = ROPE_THETA ** (-jnp.arange(0, ROT_DIM, 2, dtype=F32) / ROT_DIM)
    ang = pos.astype(F32)[:, None] * inv[None, :]
    cos, sin = jnp.cos(ang), jnp.sin(ang)
    T = pos.shape[0]
    one = jnp.ones((T, DIFF_DQK - ROT_DIM), F32)
    zero = jnp.zeros((T, DIFF_DQK - ROT_DIM), F32)
    zh = jnp.zeros((T, half), F32)
    c64 = jnp.concatenate([cos, cos, one], axis=1)
    s1_64 = jnp.concatenate([-sin, zh, zero], axis=1)
    s2_64 = jnp.concatenate([zh, sin, zero], axis=1)
    return tuple(jnp.concatenate([a, a], axis=1) for a in (c64, s1_64, s2_64))


def _lambda(l1q, l1k, l2q, l2k, lam_init):
    a = jnp.sum(l1q * l1k, axis=-1, keepdims=True)
    b = jnp.sum(l2q * l2k, axis=-1, keepdims=True)
    return jnp.exp(a) - jnp.exp(b) + lam_init


def _flash_kernel(q_ref, k_ref, v_ref, l1q_ref, l1k_ref, l2q_ref, l2k_ref, g_ref, o_ref, *, tq, lam_init):
    qi = pl.program_id(2)
    q = q_ref[...]
    lane = lax.broadcasted_iota(jnp.int32, (1, LANES), 1)
    qs = [jnp.where((lane // DIFF_DQK) == j, q, jnp.zeros_like(q)) for j in range(2)]

    def block(off, carry, keep):
        k = k_ref[pl.ds(off, tq), :]
        v = v_ref[pl.ds(off, tq), :]
        ss = [_dot_nt(k, qs[j]) for j in range(2)]
        out = []
        for j in range(2):
            m_prev, l_prev, a_prev = carry[j]
            s = ss[j] if keep is None else jnp.where(keep, ss[j], NEG)
            m_new = jnp.maximum(m_prev, jnp.max(s, axis=0, keepdims=True))
            alpha = jnp.exp(m_prev - m_new)
            p = jnp.exp(s - m_new)
            l_new = alpha * l_prev + jnp.sum(p, axis=0, keepdims=True)
            pv = lax.dot_general(v, p.astype(BF16), (((0,), (0,)), ((), ())), preferred_element_type=F32)
            out.append((m_new, l_new, alpha * a_prev + pv))
        return tuple(out)

    one = (jnp.full((1, tq), NEG, F32), jnp.zeros((1, tq), F32), jnp.zeros((DIFF_DV, tq), F32))
    carry = lax.fori_loop(0, qi, lambda ki, c: block(pl.multiple_of(ki * tq, tq), c, None), (one, one))
    rr = lax.broadcasted_iota(jnp.int32, (tq, tq), 0)
    cc = lax.broadcasted_iota(jnp.int32, (tq, tq), 1)
    (m0, l0, a0), (m1, l1, a1) = block(pl.multiple_of(qi * tq, tq), carry, rr <= cc)
    del m0, m1
    lam = _lambda(l1q_ref[...], l1k_ref[...], l2q_ref[...], l2k_ref[...], lam_init)
    o = a0 / l0 - lam * (a1 / l1)
    o = o * lax.rsqrt(jnp.mean(o * o, axis=0, keepdims=True) + EPS) * g_ref[...]
    o_ref[...] = jnp.transpose(o * (1.0 - lam_init)).astype(o_ref.dtype)


def _flash_attention(q, k, v, lparams, g, B, T, lam_init):
    tq = min(T, 512)
    nq = T // tq
    vec64 = pl.BlockSpec((1, DIFF_DQK), lambda b_, h, qi: (0, 0))
    kv = pl.BlockSpec((T, LANES), lambda b_, h, qi: (b_, h))
    return pl.pallas_call(
        functools.partial(_flash_kernel, tq=tq, lam_init=lam_init),
        grid=(B, DIFF_HEADS, nq),
        in_specs=[pl.BlockSpec((tq, LANES), lambda b_, h, qi: (b_ * nq + qi, h)), kv, kv,
                  vec64, vec64, vec64, vec64,
                  pl.BlockSpec((DIFF_DV, 1), lambda b_, h, qi: (0, 0))],
        out_specs=pl.BlockSpec((tq, LANES), lambda b_, h, qi: (b_ * nq + qi, h)),
        out_shape=SDS((B * T, DIFF_HEADS * DIFF_DV), BF16),
        compiler_params=_cparams("parallel", "parallel", "arbitrary"),
        name="flash_diff_attention",
    )(q, k, v, *lparams, g.reshape(DIFF_DV, 1))


def _paged_kernel(pt_ref, q_ref, kn_ref, vn_ref, l1q_ref, l1k_ref, l2q_ref, l2k_ref, g_ref, e_ref, *rest,
                  pps, nsteps, tnew, lam_init):
    del pt_ref
    kv_refs = rest[:2 * pps]
    o_ref, m_sc, l_sc, acc_sc = rest[2 * pps:]
    s_id = pl.program_id(1)
    nsub = 2 * DIFF_HEADS
    nrow = nsub * tnew
    kdim = nsub * DIFF_DQK

    @pl.when(s_id == 0)
    def _():
        m_sc[...] = jnp.full(m_sc.shape, NEG, F32)
        l_sc[...] = jnp.zeros(l_sc.shape, F32)
        acc_sc[...] = jnp.zeros(acc_sc.shape, F32)

    ntok = kn_ref.shape[-1]
    er = lax.broadcasted_iota(jnp.int32, (nrow, ntok * DIFF_HEADS), 0)
    ec = lax.broadcasted_iota(jnp.int32, (nrow, ntok * DIFF_HEADS), 1)
    own_head = (ec % DIFF_HEADS) == (er // (2 * tnew))

    def update(k_refs, v_refs, keep):
        ss = [_dot(q_ref[0], k[...].reshape(kdim, ntok).astype(BF16)) for k in k_refs]
        if keep is not None:
            ss = [jnp.where(keep, s, NEG) for s in ss]
        m_prev = m_sc[...]
        m_new = m_prev
        for s in ss:
            m_new = jnp.maximum(m_new, jnp.max(s, axis=-1, keepdims=True))
        alpha = jnp.exp(m_prev - m_new)
        l_new = alpha * l_sc[...]
        acc = alpha * acc_sc[...]
        for s, v in zip(ss, v_refs):
            p = jnp.exp(s - m_new)
            if keep is not None:
                p = jnp.where(keep, p, 0.0)
            l_new = l_new + jnp.sum(p, axis=-1, keepdims=True)
            p_wide = _dot(p.astype(BF16), e_ref[...])
            p_bd = jnp.where(own_head, p_wide, 0.0).astype(BF16)
            acc = acc + _dot(p_bd, v[...].reshape(ntok * DIFF_HEADS, DIFF_DV).astype(BF16))
        l_sc[...] = l_new
        acc_sc[...] = acc
        m_sc[...] = m_new

    @pl.when(s_id < nsteps - 1)
    def _():
        update(kv_refs[:pps], kv_refs[pps:], None)

    @pl.when(s_id == nsteps - 1)
    def _():
        rr = lax.broadcasted_iota(jnp.int32, (nrow, ntok), 0)
        cc = lax.broadcasted_iota(jnp.int32, (nrow, ntok), 1)
        update([kn_ref.at[0]], [vn_ref.at[0]], cc <= (rr % tnew))
        lam = _lambda(l1q_ref[...], l1k_ref[...], l2q_ref[...], l2k_ref[...], lam_init)
        o = acc_sc[...] / l_sc[...]
        rows = 2 * tnew
        for hv in range(DIFF_HEADS):
            o1 = o[hv * rows:hv * rows + tnew]
            o2 = o[hv * rows + tnew:(hv + 1) * rows]
            d = o1 - lam * o2
            d = d * lax.rsqrt(jnp.mean(d * d, axis=-1, keepdims=True) + EPS) * g_ref[...]
            o_ref[0, hv * tnew:(hv + 1) * tnew, :] = d * (1.0 - lam_init)


def _paged_attention(q_bd, k_new_t, v_new, cache_kt, cache_v, page_table, layer, lparams, g, lam_init, tnew):
    DB, n_pages = page_table.shape
    page = cache_v.shape[2]
    pps = 8 if n_pages % 8 == 0 else 1
    nsteps = n_pages // pps + 1
    nsub = 2 * DIFF_HEADS
    nrow = nsub * tnew

    def kv_spec(r, shape):
        def imap(b_, s, pt):
            idx = jnp.minimum(s * pps + r, n_pages - 1)
            return (layer, pt[b_ * n_pages + idx], 0, 0, 0)
        return pl.BlockSpec((None, None) + shape, imap)

    vec64 = pl.BlockSpec((1, DIFF_DQK), lambda b_, s, pt: (0, 0))
    in_specs = [pl.BlockSpec((1, nrow, nsub * DIFF_DQK), lambda b_, s, pt: (b_, 0, 0)),
                pl.BlockSpec((1, nsub, DIFF_DQK, page), lambda b_, s, pt: (b_, 0, 0, 0)),
                pl.BlockSpec((1, page, DIFF_HEADS, DIFF_DV), lambda b_, s, pt: (b_, 0, 0, 0)),
                vec64, vec64, vec64, vec64,
                pl.BlockSpec((1, DIFF_DV), lambda b_, s, pt: (0, 0)),
                pl.BlockSpec((page, page * DIFF_HEADS), lambda b_, s, pt: (0, 0))]
    spread = (jnp.arange(page * DIFF_HEADS)[None, :] // DIFF_HEADS == jnp.arange(page)[:, None]).astype(BF16)
    in_specs += [kv_spec(r, (nsub, DIFF_DQK, page)) for r in range(pps)]
    in_specs += [kv_spec(r, (page, DIFF_HEADS, DIFF_DV)) for r in range(pps)]
    grid_spec = pltpu.PrefetchScalarGridSpec(
        num_scalar_prefetch=1,
        grid=(DB, nsteps),
        in_specs=in_specs,
        out_specs=pl.BlockSpec((1, DIFF_HEADS * tnew, DIFF_DV), lambda b_, s, pt: (b_, 0, 0)),
        scratch_shapes=[pltpu.VMEM((nrow, 1), F32), pltpu.VMEM((nrow, 1), F32),
                        pltpu.VMEM((nrow, DIFF_DV), F32)])
    return pl.pallas_call(
        functools.partial(_paged_kernel, pps=pps, nsteps=nsteps, tnew=tnew, lam_init=lam_init),
        grid_spec=grid_spec,
        out_shape=SDS((DB, DIFF_HEADS * tnew, DIFF_DV), F32),
        compiler_params=_cparams("parallel", "arbitrary"),
        name="paged_diff_attention",
    )(page_table.reshape(-1), q_bd, k_new_t, v_new, *lparams, g, spread,
      *([cache_kt] * pps), *([cache_v] * pps))


def _outproj_kernel(a_ref, b_ref, c_ref, wa_ref, wb_ref, wc_ref, x_ref, o_ref):
    acc = _dot(a_ref[...].astype(BF16), wa_ref[...])
    acc = acc + _dot(b_ref[...].astype(BF16), wb_ref[...])
    acc = acc + _dot(c_ref[...].astype(BF16), wc_ref[...])
    o_ref[...] = x_ref[...] + acc


def _out_proj(conf, odn, att, w_out, x):
    n = x.shape[0]
    tm = min(n, 1024)
    tn = 512
    return pl.pallas_call(
        _outproj_kernel,
        grid=(n // tm, D_MODEL // tn),
        in_specs=[pl.BlockSpec((tm, 512), lambda i, j: (i, 0)),
                  pl.BlockSpec((tm, 512), lambda i, j: (i, 0)),
                  pl.BlockSpec((tm, 1024), lambda i, j: (i, 0)),
                  pl.BlockSpec((512, tn), lambda i, j: (0, j)),
                  pl.BlockSpec((512, tn), lambda i, j: (1, j)),
                  pl.BlockSpec((1024, tn), lambda i, j: (1, j)),
                  pl.BlockSpec((tm, tn), lambda i, j: (i, j))],
        out_specs=pl.BlockSpec((tm, tn), lambda i, j: (i, j)),
        out_shape=SDS((n, D_MODEL), F32),
        compiler_params=_cparams("parallel", "arbitrary"),
        name="out_proj",
    )(conf, odn, att, w_out, w_out, w_out, x)


META_GROUP = N_EXPERTS
META_RANK = N_EXPERTS + 1
ROUTED_W = D_MODEL + LANES


def _router_kernel(x_ref, g_ref, wr_ref, br_ref, hg_ref, cnt_ref, carry):
    i = pl.program_id(0)

    @pl.when(i == 0)
    def _():
        carry[...] = jnp.zeros(carry.shape, F32)

    x = x_ref[...]
    tm = x.shape[0]
    y = x * lax.rsqrt(jnp.mean(x * x, axis=-1, keepdims=True) + EPS)
    h = y * g_ref[...]
    hg_ref[:, :D_MODEL] = h
    logits = _mm3(h, wr_ref[...]) + br_ref[...]
    lane = lax.broadcasted_iota(jnp.int32, logits.shape, 1)
    big = jnp.int32(1 << 20)
    is_g = (lane >= N_EXPERTS) & (lane < N_EXPERTS + N_GROUPS)
    gl = jnp.where(is_g, logits, NEG)
    gmax = jnp.max(gl, axis=-1, keepdims=True)
    gsel = jnp.min(jnp.where(is_g & (gl == gmax), lane, big), axis=-1, keepdims=True) - N_EXPERTS
    gw = 1.0 / jnp.sum(jnp.where(is_g, jnp.exp(gl - gmax), 0.0), axis=-1, keepdims=True)
    in_grp = (lane >= gsel * EXPERTS_PER_GROUP) & (lane < (gsel + 1) * EXPERTS_PER_GROUP)
    el = jnp.where(in_grp, logits, NEG)
    v1 = jnp.max(el, axis=-1, keepdims=True)
    i1 = jnp.min(jnp.where(in_grp & (el == v1), lane, big), axis=-1, keepdims=True)
    rest = in_grp & (lane != i1)
    el2 = jnp.where(rest, logits, NEG)
    v2 = jnp.max(el2, axis=-1, keepdims=True)
    i2 = jnp.min(jnp.where(rest & (el2 == v2), lane, big), axis=-1, keepdims=True)
    e2 = jnp.exp(v2 - v1)
    den = 1.0 + e2
    gates = jnp.where(lane == i1, (1.0 / den) * gw, 0.0) + jnp.where(lane == i2, (e2 / den) * gw, 0.0)
    onehot = jnp.where(lane == gsel, 1.0, 0.0)
    rr = lax.broadcasted_iota(jnp.int32, (tm, tm), 0)
    cc = lax.broadcasted_iota(jnp.int32, (tm, tm), 1)
    before = _dot(jnp.where(cc < rr, 1.0, 0.0).astype(BF16), onehot.astype(BF16)) + carry[...]
    rank = jnp.sum(onehot * before, axis=-1, keepdims=True)
    hg_ref[:, D_MODEL:] = (gates + jnp.where(lane == META_GROUP, gsel.astype(F32), 0.0)
                           + jnp.where(lane == META_RANK, rank, 0.0))
    carry[...] = carry[...] + jnp.sum(onehot, axis=0, keepdims=True)
    cnt_ref[...] = carry[...]


def _router(x, g, wr, br):
    n = x.shape[0]
    tm = min(n, 512)
    return pl.pallas_call(
        _router_kernel,
        grid=(n // tm,),
        in_specs=[pl.BlockSpec((tm, D_MODEL), lambda i: (i, 0)),
                  pl.BlockSpec((1, D_MODEL), lambda i: (0, 0)),
                  pl.BlockSpec((D_MODEL, LANES), lambda i: (0, 0)),
                  pl.BlockSpec((1, LANES), lambda i: (0, 0))],
        out_specs=[pl.BlockSpec((tm, ROUTED_W), lambda i: (i, 0)),
                   pl.BlockSpec((1, LANES), lambda i: (0, 0))],
        out_shape=[SDS((n, ROUTED_W), F32), SDS((1, LANES), F32)],
        scratch_shapes=[pltpu.VMEM((1, LANES), F32)],
        compiler_params=_cparams("arbitrary"),
        name="norm_router",
    )(x, g, wr, br)


def _route_plan(hg, counts, tile, n_tiles):
    cnt = counts[0, :N_GROUPS].astype(jnp.int32)
    size = ((cnt + tile - 1) // tile) * tile
    start = jnp.cumsum(size) - size
    grp = hg[:, D_MODEL + META_GROUP].astype(jnp.int32)
    rank = hg[:, D_MODEL + META_RANK].astype(jnp.int32)
    dest = rank
    for k in range(N_GROUPS):
        dest = dest + jnp.where(grp == k, start[k], 0)
    tile_start = jnp.arange(n_tiles, dtype=jnp.int32) * tile
    tile_group = jnp.sum((tile_start[:, None] >= start[None, :]).astype(jnp.int32), axis=1) - 1
    n_used = (jnp.sum(size) // tile).astype(jnp.int32).reshape(1)
    return dest, jnp.clip(tile_group, 0, N_GROUPS - 1), n_used


def _row_copy(src_ref, src_row, dst_ref, dst_row, sem):
    return pltpu.make_async_copy(src_ref.at[pl.ds(src_row, 1), :], dst_ref.at[pl.ds(dst_row, 1), :], sem)


def _dispatch_kernel(dest_ref, hg_ref, xs_in_ref, xs_ref, sem, *, rows):
    del xs_in_ref
    base = pl.program_id(0) * rows

    def issue(r, c):
        _row_copy(hg_ref, r, xs_ref, dest_ref[base + r], sem).start()
        return c

    def drain(r, c):
        _row_copy(hg_ref, 0, xs_ref, 0, sem).wait()
        return c

    lax.fori_loop(0, rows, issue, 0, unroll=8)
    lax.fori_loop(0, rows, drain, 0, unroll=8)


def _dispatch(dest, hg, n_rows):
    n = hg.shape[0]
    rows = min(n, 256)
    grid_spec = pltpu.PrefetchScalarGridSpec(
        num_scalar_prefetch=1, grid=(n // rows,),
        in_specs=[pl.BlockSpec((rows, ROUTED_W), lambda i, d: (i, 0)),
                  pl.BlockSpec(memory_space=pl.ANY)],
        out_specs=pl.BlockSpec(memory_space=pl.ANY),
        scratch_shapes=[pltpu.SemaphoreType.DMA(())])
    return pl.pallas_call(
        functools.partial(_dispatch_kernel, rows=rows),
        grid_spec=grid_spec,
        out_shape=SDS((n_rows, ROUTED_W), F32),
        input_output_aliases={2: 0},
        compiler_params=_cparams("arbitrary"),
        name="moe_dispatch",
    )(dest, hg, jnp.zeros((n_rows, ROUTED_W), F32))


def _experts_kernel(tg_ref, nu_ref, xs_ref, wg_ref, wu_ref, wd_ref, o_ref):
    i = pl.program_id(0)
    e = pl.program_id(1)

    @pl.when(e == 0)
    def _():
        o_ref[...] = jnp.zeros(o_ref.shape, F32)

    @pl.when(i < nu_ref[0])
    def _():
        h = xs_ref[:, :D_MODEL].astype(BF16)
        meta = xs_ref[:, D_MODEL:]
        lane = lax.broadcasted_iota(jnp.int32, meta.shape, 1)
        col = tg_ref[i] * EXPERTS_PER_GROUP + e
        ge = jnp.sum(jnp.where(lane == col, meta, 0.0), axis=-1, keepdims=True)
        act = _silu(_dot(h, wg_ref[...])) * _dot(h, wu_ref[...]) * ge
        o_ref[...] += _dot(act.astype(BF16), wd_ref[...])


def _experts(tile_group, n_used, xs, wg, wu, wd, layer, tile):
    n_rows = xs.shape[0]
    wspec = lambda shape: pl.BlockSpec(
        (None, None) + shape, lambda i, e, tg, nu: (layer, tg[i] * EXPERTS_PER_GROUP + e, 0, 0))
    grid_spec = pltpu.PrefetchScalarGridSpec(
        num_scalar_prefetch=2, grid=(n_rows // tile, EXPERTS_PER_GROUP),
        in_specs=[pl.BlockSpec((tile, ROUTED_W), lambda i, e, tg, nu: (i, 0)),
                  wspec((D_MODEL, D_EXPERT)), wspec((D_MODEL, D_EXPERT)), wspec((D_EXPERT, D_MODEL))],
        out_specs=pl.BlockSpec((tile, D_MODEL), lambda i, e, tg, nu: (i, 0)))
    return pl.pallas_call(
        _experts_kernel,
        grid_spec=grid_spec,
        out_shape=SDS((n_rows, D_MODEL), F32),
        compiler_params=_cparams("parallel", "arbitrary"),
        name="moe_experts",
    )(tile_group, n_used, xs, wg, wu, wd)


def _combine_kernel(dest_ref, x_ref, ys_ref, o_ref, buf, sem, *, rows):
    base = pl.program_id(0) * rows

    def issue(r, c):
        _row_copy(ys_ref, dest_ref[base + r], buf, r, sem).start()
        return c

    def drain(r, c):
        _row_copy(ys_ref, 0, buf, 0, sem).wait()
        return c

    lax.fori_loop(0, rows, issue, 0, unroll=8)
    lax.fori_loop(0, rows, drain, 0, unroll=8)
    o_ref[...] = x_ref[...] + buf[...]


def _combine(dest, x, ys):
    n = x.shape[0]
    rows = min(n, 256)
    grid_spec = pltpu.PrefetchScalarGridSpec(
        num_scalar_prefetch=1, grid=(n // rows,),
        in_specs=[pl.BlockSpec((rows, D_MODEL), lambda i, d: (i, 0)),
                  pl.BlockSpec(memory_space=pl.ANY)],
        out_specs=pl.BlockSpec((rows, D_MODEL), lambda i, d: (i, 0)),
        scratch_shapes=[pltpu.VMEM((rows, D_MODEL), F32), pltpu.SemaphoreType.DMA(())])
    return pl.pallas_call(
        functools.partial(_combine_kernel, rows=rows),
        grid_spec=grid_spec,
        out_shape=SDS((n, D_MODEL), F32),
        compiler_params=_cparams("arbitrary"),
        name="moe_combine",
    )(dest, x, ys)


def _moe_block(x, g, wr, br, wg, wu, wd, layer):
    n = x.shape[0]
    tile = min(n, 512)
    n_tiles = n // tile + N_GROUPS
    hg, counts = _router(x, g, wr, br)
    dest, tile_group, n_used = _route_plan(hg, counts, tile, n_tiles)
    xs = _dispatch(dest, hg, n_tiles * tile)
    ys = _experts(tile_group, n_used, xs, wg, wu, wd, layer, tile)
    return _combine(dest, x, ys)


def _row(v, width=None):
    v = v.astype(F32).reshape(1, -1)
    if width is not None and v.shape[1] < width:
        v = jnp.pad(v, ((0, 0), (0, width - v.shape[1])))
    return v


def _prep_layer(l, w):
    wi = w['w_in'][l]
    w_main = jnp.concatenate([wi[:, 1024:2560], wi[:, 2560:3072], wi[:, 0:1024], wi[:, 3080:6152]],
                             axis=1).astype(BF16)
    w_ba = jnp.pad(wi[:, 3072:3080], ((0, 0), (0, LANES - 8))).astype(BF16)
    wr = jnp.concatenate([w['router_expert_w'][l], w['router_group_w'][l]], axis=1)
    wr = jnp.pad(wr, ((0, 0), (0, LANES - wr.shape[1])))
    br = _row(jnp.concatenate([w['router_expert_b'][l], w['router_group_b'][l]]), LANES)
    lane_pad = lambda v: _row(jnp.concatenate([jnp.zeros((DN_HEADS,), F32), v.astype(F32)]), LANES)
    return dict(
        norm1_g=_row(w['norm1_g'][l]), w_main=w_main, w_ba=w_ba,
        conf_w=w['conf_dw_w'][l], conf_b=_row(w['conf_dw_b'][l]),
        conf_g=_row(w['conf_ln_g'][l]), conf_bb=_row(w['conf_ln_b'][l]),
        dn_cw=w['dn_conv_w'][l], dn_av=lane_pad(w['dn_a_log'][l]), dn_dv=lane_pad(w['dn_dt_bias'][l]),
        dn_ng=_row(w['dn_norm_g'][l]),
        gq=_row(jnp.tile(w['diff_q_norm_g'][l], 2)), gk=_row(jnp.tile(w['diff_k_norm_g'][l], 2)),
        lparams=tuple(_row(w[k][l]) for k in ('diff_lambda_q1', 'diff_lambda_k1',
                                               'diff_lambda_q2', 'diff_lambda_k2')),
        subln=_row(w['diff_subln_g'][l]),
        w_out=w['w_out'][l].astype(BF16), norm2_g=_row(w['norm2_g'][l]), wr=wr, br=br,
        layer=l)


def _layer(x, B, T, tables, conf_prev, dnc_prev, dn_s0, lp, moe_w, lam_init, paged):
    n = B * T
    act_dtype = BF16 if T % 16 == 0 else F32
    p, ba = _in_proj(x, lp['norm1_g'], lp['w_main'], lp['w_ba'])
    conf, conf_new = _conformer(p, conf_prev, lp['conf_w'], lp['conf_b'], lp['conf_g'], lp['conf_bb'],
                                B, T, act_dtype)
    odn, dnc_new, dn_state = _deltanet(p, ba, dnc_prev, lp['dn_cw'], lp['dn_av'], lp['dn_dv'],
                                       lp['dn_ng'], dn_s0, B, T, act_dtype)
    qn, kf, kb, vb = _qknorm(p, lp['gq'], lp['gk'], *tables, n, T, act_dtype)
    v_f32 = p[:, P_DV:P_DV + DIFF_HEADS * DIFF_DV]
    k_out = kf.reshape(B, T, 2 * DIFF_HEADS, DIFF_DQK)
    v_out = v_f32.reshape(B, T, DIFF_HEADS, DIFF_DV)
    if paged is None:
        att = _flash_attention(qn, kb, vb, lp['lparams'], lp['subln'], B, T, lam_init)
    else:
        cache_kt, cache_v, page_table, layer = paged
        page = cache_v.shape[2]
        nsub = 2 * DIFF_HEADS
        q4 = qn.reshape(B, T, nsub, DIFF_DQK).transpose(0, 2, 1, 3)
        eye = jnp.eye(nsub, dtype=qn.dtype)
        q_bd = (q4[:, :, :, None, :] * eye[None, :, None, :, None]).reshape(B, nsub * T, nsub * DIFF_DQK)
        k_new_t = jnp.pad(k_out.transpose(0, 2, 3, 1), ((0, 0), (0, 0), (0, 0), (0, page - T)))
        v_new = jnp.pad(v_out, ((0, 0), (0, page - T), (0, 0), (0, 0)))
        att = _paged_attention(q_bd.astype(BF16), k_new_t, v_new, cache_kt, cache_v, page_table, layer,
                               lp['lparams'], lp['subln'], lam_init, T)
        att = att.reshape(B, DIFF_HEADS, T, DIFF_DV).transpose(0, 2, 1, 3).reshape(n, DIFF_HEADS * DIFF_DV)
    x2 = _out_proj(conf, odn, att, lp['w_out'], x)
    x3 = _moe_block(x2, lp['norm2_g'], lp['wr'], lp['br'], *moe_w, lp['layer'])
    return x3, (k_out, v_out, dn_state, dnc_new, conf_new)


def kernel(x_prompt, x_sample, cache_k, cache_v, page_table, state_delta, state_delta_conv, state_conf_conv, norm1_g, w_in, conf_dw_w, conf_dw_b, conf_ln_g, conf_ln_b, dn_conv_w, dn_a_log, dn_dt_bias, dn_norm_g, diff_q_norm_g, diff_k_norm_g, diff_lambda_q1, diff_lambda_k1, diff_lambda_q2, diff_lambda_k2, diff_subln_g, w_out, norm2_g, router_group_w, router_group_b, router_expert_w, router_expert_b, moe_w_gate, moe_w_up, moe_w_down):
    w = dict(norm1_g=norm1_g, w_in=w_in, conf_dw_w=conf_dw_w, conf_dw_b=conf_dw_b, conf_ln_g=conf_ln_g,
             conf_ln_b=conf_ln_b, dn_conv_w=dn_conv_w, dn_a_log=dn_a_log, dn_dt_bias=dn_dt_bias,
             dn_norm_g=dn_norm_g, diff_q_norm_g=diff_q_norm_g, diff_k_norm_g=diff_k_norm_g,
             diff_lambda_q1=diff_lambda_q1, diff_lambda_k1=diff_lambda_k1, diff_lambda_q2=diff_lambda_q2,
             diff_lambda_k2=diff_lambda_k2, diff_subln_g=diff_subln_g, w_out=w_out, norm2_g=norm2_g,
             router_group_w=router_group_w, router_group_b=router_group_b,
             router_expert_w=router_expert_w, router_expert_b=router_expert_b,
             moe_w_gate=moe_w_gate, moe_w_up=moe_w_up, moe_w_down=moe_w_down)
    B, S, _ = x_prompt.shape
    DB, T, _ = x_sample.shape
    depth = w_in.shape[0]
    past_len = page_table.shape[1] * cache_k.shape[2]
    cache_kt = jnp.transpose(cache_k, (0, 1, 3, 4, 2))
    tab_p = _rope_tables(jnp.arange(S, dtype=jnp.int32))
    tab_s = tuple(jnp.tile(a, (DB, 1)) for a in _rope_tables(past_len + jnp.arange(T, dtype=jnp.int32)))
    hp = x_prompt.reshape(B * S, D_MODEL)
    hs = x_sample.reshape(DB * T, D_MODEL)
    zero_conf = jnp.zeros((B, CONF_K - 1, CONF_W), F32)
    zero_dnc = jnp.zeros((B, DN_CONV - 1, DN_QKV), F32)
    zero_s = jnp.zeros((B, DN_HEADS, DN_DK, DN_DV), F32)
    moe_w = (moe_w_gate.astype(BF16), moe_w_up.astype(BF16), moe_w_down.astype(BF16))
    outs_p, outs_s = [], []
    for l in range(depth):
        lp = _prep_layer(l, w)
        lam_init = 0.8 - 0.6 * math.exp(-0.3 * l)
        hp, sp = _layer(hp, B, S, tab_p, zero_conf, zero_dnc, zero_s, lp, moe_w, lam_init, None)
        hs, ss = _layer(hs, DB, T, tab_s, state_conf_conv[l], state_delta_conv[l], state_delta[l], lp,
                        moe_w, lam_init, (cache_kt, cache_v, page_table, l))
        outs_p.append(sp)
        outs_s.append(ss)
    st = lambda outs, i: jnp.stack([o[i] for o in outs])
    return (hp.reshape(B, S, D_MODEL), hs.reshape(DB, T, D_MODEL),
            st(outs_p, 0), st(outs_p, 1), st(outs_s, 0), st(outs_s, 1),
            st(outs_p, 2), st(outs_s, 2), st(outs_p, 3), st(outs_s, 3),
            st(outs_p, 4), st(outs_s, 4))
```

```python
import functools
import math

import jax
import jax.numpy as jnp
from jax import lax
from jax.experimental import pallas as pl
from jax.experimental.pallas import tpu as pltpu

F32 = jnp.float32
BF16 = jnp.bfloat16
SDS = jax.ShapeDtypeStruct

D_MODEL = 2048
CONF_W = 512
CONF_K = 31
DN_HEADS = 4
DN_DK = 128
DN_DV = 128
DN_CONV = 4
DN_CHUNK = 64
DN_QKV = DN_HEADS * (2 * DN_DK + DN_DV)
DIFF_HEADS = 8
DIFF_DQK = 64
DIFF_DV = 128
ROT_DIM = 16
ROPE_THETA = 500000.0
N_GROUPS = 4
EXPERTS_PER_GROUP = 4
N_EXPERTS = 16
D_EXPERT = 512
EPS = 1e-6
LANES = 128
VMEM_LIMIT = 48 * 1024 * 1024

P_QKV, P_Z, P_GLU, P_DQ, P_DK, P_DV = 0, 1536, 2048, 3072, 4096, 5120
P_W = 6144
NEG = -1e30


def _cparams(*sem):
    return pltpu.CompilerParams(dimension_semantics=sem, vmem_limit_bytes=VMEM_LIMIT)


def _dot(a, b):
    return jnp.dot(a, b, preferred_element_type=F32)


def _dot_nt(a, b):
    return lax.dot_general(a, b, (((1,), (1,)), ((), ())), preferred_element_type=F32)


def _split(a):
    hi = a.astype(BF16)
    return hi, (a - hi.astype(F32)).astype(BF16)


def _mm3(a, b):
    ah, al = _split(a)
    bh, bl = _split(b)
    return _dot(ah, bh) + (_dot(ah, bl) + _dot(al, bh))


def _mm3_nt(a, b):
    ah, al = _split(a)
    bh, bl = _split(b)
    return _dot_nt(ah, bh) + (_dot_nt(ah, bl) + _dot_nt(al, bh))


def _sigmoid(x):
    return 1.0 / (1.0 + jnp.exp(-x))


def _silu(x):
    return x * _sigmoid(x)


def _softplus(x):
    return jnp.maximum(x, 0.0) + jnp.log1p(jnp.exp(-jnp.abs(x)))


def _inproj_kernel(x_ref, g_ref, w_ref, wba_ref, o_ref, oba_ref, h_ref):
    @pl.when(pl.program_id(1) == 0)
    def _():
        x = x_ref[...]
        y = x * lax.rsqrt(jnp.mean(x * x, axis=-1, keepdims=True) + EPS)
        h_ref[...] = (y * g_ref[...]).astype(BF16)
        oba_ref[...] = _dot(h_ref[...], wba_ref[...])

    o_ref[...] = _dot(h_ref[...], w_ref[...])


def _in_proj(x, g, w_main, w_ba):
    n = x.shape[0]
    tm = min(n, 1024)
    tn = 1024
    return pl.pallas_call(
        _inproj_kernel,
        grid=(n // tm, P_W // tn),
        in_specs=[pl.BlockSpec((tm, D_MODEL), lambda i, j: (i, 0)),
                  pl.BlockSpec((1, D_MODEL), lambda i, j: (0, 0)),
                  pl.BlockSpec((D_MODEL, tn), lambda i, j: (0, j)),
                  pl.BlockSpec((D_MODEL, LANES), lambda i, j: (0, 0))],
        out_specs=[pl.BlockSpec((tm, tn), lambda i, j: (i, j)),
                   pl.BlockSpec((tm, LANES), lambda i, j: (i, 0))],
        out_shape=[SDS((n, P_W), F32), SDS((n, LANES), F32)],
        scratch_shapes=[pltpu.VMEM((tm, D_MODEL), BF16)],
        compiler_params=_cparams("parallel", "arbitrary"),
        name="in_proj",
    )(x, g, w_main, w_ba)


def _conf_kernel(p_ref, prev_ref, w_ref, b_ref, g_ref, bb_ref, o_ref, new_ref, ubuf, *, tt, nt):
    t = pl.program_id(1)
    hist = 32

    @pl.when(t == 0)
    def _():
        ubuf[0:2, :] = jnp.zeros((2, CONF_W), F32)
        ubuf[2:hist, :] = prev_ref[0]

    if nt > 1:
        @pl.when(t > 0)
        def _():
            ubuf[0:hist, :] = ubuf[tt:tt + hist, :]

    x = p_ref[...]
    ubuf[hist:hist + tt, :] = x[:, :CONF_W] * _sigmoid(x[:, CONF_W:])
    rc = min(tt, 64)
    for r0 in range(0, tt, rc):
        acc = jnp.zeros((rc, CONF_W), F32)
        for j in range(CONF_K):
            s = r0 + hist - (CONF_K - 1) + j
            acc = acc + w_ref[j:j + 1, :] * ubuf[s:s + rc, :]
        c = acc + b_ref[...]
        xc = c - jnp.mean(c, axis=-1, keepdims=True)
        y = xc * lax.rsqrt(jnp.mean(xc * xc, axis=-1, keepdims=True) + EPS)
        y = y * g_ref[...] + bb_ref[...]
        o_ref[r0:r0 + rc, :] = _silu(y).astype(o_ref.dtype)

    @pl.when(t == nt - 1)
    def _():
        new_ref[0] = ubuf[tt + hist - (CONF_K - 1):tt + hist, :]


def _conformer(p, prev, w, b, g, bb, B, T, act_dtype):
    tt = min(T, 256)
    nt = T // tt
    return pl.pallas_call(
        functools.partial(_conf_kernel, tt=tt, nt=nt),
        grid=(B, nt),
        in_specs=[pl.BlockSpec((tt, 2 * CONF_W), lambda b_, t: (b_ * nt + t, P_GLU // (2 * CONF_W))),
                  pl.BlockSpec((1, CONF_K - 1, CONF_W), lambda b_, t: (b_, 0, 0)),
                  pl.BlockSpec((CONF_K, CONF_W), lambda b_, t: (0, 0)),
                  pl.BlockSpec((1, CONF_W), lambda b_, t: (0, 0)),
                  pl.BlockSpec((1, CONF_W), lambda b_, t: (0, 0)),
                  pl.BlockSpec((1, CONF_W), lambda b_, t: (0, 0))],
        out_specs=[pl.BlockSpec((tt, CONF_W), lambda b_, t: (b_ * nt + t, 0)),
                   pl.BlockSpec((1, CONF_K - 1, CONF_W), lambda b_, t: (b_, 0, 0))],
        out_shape=[SDS((B * T, CONF_W), act_dtype), SDS((B, CONF_K - 1, CONF_W), F32)],
        scratch_shapes=[pltpu.VMEM((tt + 32, CONF_W), F32)],
        compiler_params=_cparams("parallel", "arbitrary"),
        name="conformer",
    )(p, prev, w, b, g, bb)


def _stack_heads(a):
    return jnp.concatenate([a[:, h * LANES:(h + 1) * LANES] for h in range(DN_HEADS)], axis=0)


def _unstack_heads(a, c):
    return jnp.concatenate([a[h * c:(h + 1) * c, :] for h in range(DN_HEADS)], axis=1)


def _stack_cols(a, lane0):
    return jnp.concatenate([a[:, lane0 + h:lane0 + h + 1] for h in range(DN_HEADS)], axis=0)


def _diag_blocks(x, rowhead):
    out = jnp.where(rowhead == 0, x[:, 0:LANES], 0.0)
    for h in range(1, DN_HEADS):
        out = out + jnp.where(rowhead == h, x[:, h * LANES:(h + 1) * LANES], 0.0)
    return out


def _unit_lower_inverse(lmat, ri, ci, mm1):
    eye = (ri == ci).astype(F32)
    d = jnp.where((ri // 16) == (ci // 16), lmat, 0.0)
    d2 = mm1(d, d)
    d4 = mm1(d2, d2)
    d8 = mm1(d4, d4)
    t = mm1(mm1(mm1(eye - d, eye + d2), eye + d4), eye + d8)
    m32 = jnp.where(((ri // 32) == (ci // 32)) & ((ri // 16) != (ci // 16)), lmat, 0.0)
    t = t - mm1(mm1(t, m32), t)
    m64 = jnp.where((ri // 32) != (ci // 32), lmat, 0.0)
    return t - mm1(mm1(t, m64), t)


def _dn_kernel(qkv_ref, z_ref, ba_ref, prev_ref, cw_ref, av_ref, dv_ref, ng_ref, s0_ref,
               o_ref, cnew_ref, snew_ref, cbuf, s_sc, *, tb, nb):
    t = pl.program_id(1)
    C = DN_CHUNK
    H = DN_HEADS
    HC = H * C
    npre = DN_CONV - 1
    if tb < C:
        mm, mm_nt = _mm3, _mm3_nt
    else:
        mm = lambda a, b: _dot(a.astype(BF16), b.astype(BF16))
        mm_nt = lambda a, b: _dot_nt(a.astype(BF16), b.astype(BF16))

    @pl.when(t == 0)
    def _():
        cbuf[0:8 - npre, :] = jnp.zeros((8 - npre, DN_QKV), F32)
        cbuf[8 - npre:8, :] = prev_ref[0]
        s_sc[...] = jnp.concatenate([s0_ref[0, h] for h in range(H)], axis=1)

    if nb > 1:
        @pl.when(t > 0)
        def _():
            cbuf[0:8, :] = cbuf[tb:tb + 8, :]

    cbuf[8:8 + tb, :] = qkv_ref[...]
    y = jnp.zeros((tb, DN_QKV), F32)
    for j in range(DN_CONV):
        y = y + cw_ref[j:j + 1, :] * cbuf[8 - npre + j:8 - npre + j + tb, :]
    y = _silu(y)

    @pl.when(t == nb - 1)
    def _():
        cnew_ref[0] = cbuf[8 + tb - npre:8 + tb, :]

    ba = ba_ref[...]
    beta = _sigmoid(ba)
    gg = -jnp.exp(av_ref[...]) * _softplus(ba + dv_ref[...])
    zz = z_ref[...]

    pad = C - tb if tb < C else 0
    rows = tb + pad

    def padr(a):
        if pad == 0:
            return a
        return jnp.concatenate([a, jnp.zeros((pad, a.shape[1]), a.dtype)], axis=0)

    def l2n(a, scale):
        parts = []
        for h in range(H):
            ah = a[:, h * LANES:(h + 1) * LANES]
            parts.append(ah * lax.rsqrt(jnp.sum(ah * ah, axis=-1, keepdims=True) + EPS) * scale)
        return jnp.concatenate(parts, axis=1)

    q_all = padr(l2n(y[:, 0:H * DN_DK], DN_DK ** -0.5))
    k_all = padr(l2n(y[:, H * DN_DK:2 * H * DN_DK], 1.0))
    v_all = padr(y[:, 2 * H * DN_DK:])
    beta = padr(beta)
    gg = padr(gg)

    ri = lax.broadcasted_iota(jnp.int32, (HC, HC), 0)
    ci = lax.broadcasted_iota(jnp.int32, (HC, HC), 1)
    same = (ri // C) == (ci // C)
    low = same & ((ri % C) >= (ci % C))
    strict = same & ((ri % C) > (ci % C))
    rowhead = lax.broadcasted_iota(jnp.int32, (HC, LANES), 0) // C
    tr = lax.broadcasted_iota(jnp.int32, (C, C), 0)
    tc = lax.broadcasted_iota(jnp.int32, (C, C), 1)
    tri = (tr >= tc).astype(F32)

    s_all = s_sc[...]
    outs = []
    for c in range(rows // C):
        r0 = c * C
        kst = _stack_heads(k_all[r0:r0 + C])
        qst = _stack_heads(q_all[r0:r0 + C])
        vst = _stack_heads(v_all[r0:r0 + C])
        gcm = _mm3(tri, gg[r0:r0 + C])
        bcol = _stack_cols(beta[r0:r0 + C], 0)
        gcol = _stack_cols(gcm, H)
        glast = jnp.concatenate(
            [jnp.broadcast_to(gcm[C - 1:C, H + h:H + h + 1], (C, 1)) for h in range(H)], axis=0)
        grow = jnp.transpose(jnp.broadcast_to(gcol, (HC, LANES)))[0:1, :]
        decay = jnp.where(low, jnp.exp(jnp.where(low, gcol - grow, 0.0)), 0.0)
        kb = kst * bcol
        lmat = jnp.where(strict, mm_nt(kb, kst) * decay, 0.0)
        attn = mm_nt(qst, kst) * decay
        tinv = _unit_lower_inverse(lmat, ri, ci, mm)
        egc = jnp.exp(gcol)
        sol = mm(tinv, jnp.concatenate([vst * bcol, kb * egc], axis=1))
        u = sol[:, :LANES]
        w = sol[:, LANES:]
        qd = qst * egc
        kd = kst * jnp.exp(glast - gcol)
        kdt = jnp.transpose(kd)
        egl = jnp.concatenate(
            [jnp.broadcast_to(jnp.exp(gcm[C - 1:C, H + h:H + h + 1]), (1, LANES)) for h in range(H)],
            axis=1)
        r = mm(jnp.concatenate([w, qd], axis=0), s_all)
        v_new = u - _diag_blocks(r[:HC], rowhead)
        o = _diag_blocks(r[HC:], rowhead) + mm(attn, v_new)
        vbd = jnp.concatenate([jnp.where(rowhead == h, v_new, 0.0) for h in range(H)], axis=1)
        s_all = s_all * egl + mm(kdt, vbd)
        o = o * lax.rsqrt(jnp.mean(o * o, axis=-1, keepdims=True) + EPS) * ng_ref[...]
        outs.append(_unstack_heads(o, C))
    s_sc[...] = s_all
    o_all = outs[0] if len(outs) == 1 else jnp.concatenate(outs, axis=0)
    o_ref[...] = (o_all[:tb] * _silu(zz)).astype(o_ref.dtype)

    @pl.when(t == nb - 1)
    def _():
        for h in range(H):
            snew_ref[0, h] = s_all[:, h * DN_DV:(h + 1) * DN_DV]


def _deltanet(p, ba, prev, cw, avec, dvec, ng, s0, B, T, act_dtype):
    tb = min(T, 256)
    nb = T // tb
    return pl.pallas_call(
        functools.partial(_dn_kernel, tb=tb, nb=nb),
        grid=(B, nb),
        in_specs=[pl.BlockSpec((tb, DN_QKV), lambda b_, t: (b_ * nb + t, P_QKV // DN_QKV)),
                  pl.BlockSpec((tb, 512), lambda b_, t: (b_ * nb + t, P_Z // 512)),
                  pl.BlockSpec((tb, LANES), lambda b_, t: (b_ * nb + t, 0)),
                  pl.BlockSpec((1, DN_CONV - 1, DN_QKV), lambda b_, t: (b_, 0, 0)),
                  pl.BlockSpec((DN_CONV, DN_QKV), lambda b_, t: (0, 0)),
                  pl.BlockSpec((1, LANES), lambda b_, t: (0, 0)),
                  pl.BlockSpec((1, LANES), lambda b_, t: (0, 0)),
                  pl.BlockSpec((1, DN_DV), lambda b_, t: (0, 0)),
                  pl.BlockSpec((1, DN_HEADS, DN_DK, DN_DV), lambda b_, t: (b_, 0, 0, 0))],
        out_specs=[pl.BlockSpec((tb, DN_HEADS * DN_DV), lambda b_, t: (b_ * nb + t, 0)),
                   pl.BlockSpec((1, DN_CONV - 1, DN_QKV), lambda b_, t: (b_, 0, 0)),
                   pl.BlockSpec((1, DN_HEADS, DN_DK, DN_DV), lambda b_, t: (b_, 0, 0, 0))],
        out_shape=[SDS((B * T, DN_HEADS * DN_DV), act_dtype),
                   SDS((B, DN_CONV - 1, DN_QKV), F32),
                   SDS((B, DN_HEADS, DN_DK, DN_DV), F32)],
        scratch_shapes=[pltpu.VMEM((tb + 8, DN_QKV), F32),
                        pltpu.VMEM((DN_DK, DN_HEADS * DN_DV), F32)],
        compiler_params=_cparams("parallel", "arbitrary"),
        name="deltanet",
    )(p, p, ba, prev, cw, avec, dvec, ng, s0)


def _qknorm_kernel(q_ref, k_ref, v_ref, gq_ref, gk_ref, c_ref, s1_ref, s2_ref,
                   qo_ref, kf_ref, kb_ref, vb_ref):
    ri = lax.broadcasted_iota(jnp.int32, (LANES, LANES), 0)
    ci = lax.broadcasted_iota(jnp.int32, (LANES, LANES), 1)
    seg = ((ri // DIFF_DQK) == (ci // DIFF_DQK)).astype(BF16)
    cos = c_ref[...]
    s1 = s1_ref[...]
    s2 = s2_ref[...]

    def norm_rope(x, g):
        sq = x * x
        hi = sq.astype(BF16)
        r1 = sq - hi.astype(F32)
        mid = r1.astype(BF16)
        lo = (r1 - mid.astype(F32)).astype(BF16)
        tot = _dot(hi, seg) + (_dot(mid, seg) + _dot(lo, seg))
        y = x * lax.rsqrt(tot * (1.0 / DIFF_DQK) + EPS) * g
        return y * cos + pltpu.roll(y, LANES - ROT_DIM // 2, 1) * s1 + pltpu.roll(y, ROT_DIM // 2, 1) * s2

    for c in range(2 * DIFF_HEADS * DIFF_DQK // LANES):
        sl = slice(c * LANES, (c + 1) * LANES)
        qn = norm_rope(q_ref[:, sl], gq_ref[...])
        kn = norm_rope(k_ref[:, sl], gk_ref[...])
        qo_ref[:, sl] = (qn * (DIFF_DQK ** -0.5)).astype(qo_ref.dtype)
        kf_ref[:, sl] = kn
        kb_ref[:, sl] = kn.astype(kb_ref.dtype)
    vb_ref[...] = v_ref[...].astype(vb_ref.dtype)


def _qknorm(p, gq, gk, cos, s1, s2, n, T, act_dtype):
    tm = min(T, 512) if T % 16 == 0 else n
    nrep = T // tm if T % 16 == 0 else 1
    W = 1024
    tab = pl.BlockSpec((tm, LANES), lambda i: (i % nrep, 0))
    vec = pl.BlockSpec((1, LANES), lambda i: (0, 0))
    return pl.pallas_call(
        _qknorm_kernel,
        grid=(n // tm,),
        in_specs=[pl.BlockSpec((tm, W), lambda i: (i, P_DQ // W)),
                  pl.BlockSpec((tm, W), lambda i: (i, P_DK // W)),
                  pl.BlockSpec((tm, W), lambda i: (i, P_DV // W)),
                  vec, vec, tab, tab, tab],
        out_specs=[pl.BlockSpec((tm, W), lambda i: (i, 0))] * 4,
        out_shape=[SDS((n, W), act_dtype), SDS((n, W), F32), SDS((n, W), act_dtype), SDS((n, W), act_dtype)],
        compiler_params=_cparams("parallel"),
        name="qknorm_rope",
    )(p, p, p, gq, gk, cos, s1, s2)


def _rope_tables(pos):
    half = ROT_DIM // 2
    inv = ROPE_THETA ** (-jnp.arange(0, ROT_DIM, 2, dtype=F32) / ROT_DIM)
    ang = pos.astype(F32)[:, None] * inv[None, :]
    cos, sin = jnp.cos(ang), jnp.sin(ang)
    T = pos.shape[0]
    one = jnp.ones((T, DIFF_DQK - ROT_DIM), F32)
    zero = jnp.zeros((T, DIFF_DQK - ROT_DIM), F32)
    zh = jnp.zeros((T, half), F32)
    c64 = jnp.concatenate([cos, cos, one], axis=1)
    s1_64 = jnp.concatenate([-sin, zh, zero], axis=1)
    s2_64 = jnp.concatenate([zh, sin, zero], axis=1)
    return tuple(jnp.concatenate([a, a], axis=1) for a in (c64, s1_64, s2_64))


def _lambda(l1q, l1k, l2q, l2k, lam_init):
    a = jnp.sum(l1q * l1k, axis=-1, keepdims=True)
    b = jnp.sum(l2q * l2k, axis=-1, keepdims=True)
    return jnp.exp(a) - jnp.exp(b) + lam_init


def _flash_kernel(q_ref, k_ref, v_ref, l1q_ref, l1k_ref, l2q_ref, l2k_ref, g_ref, o_ref, *, tq, lam_init):
    qi = pl.program_id(2)
    q = q_ref[...]
    lane = lax.broadcasted_iota(jnp.int32, (1, LANES), 1)
    qs = [jnp.where((lane // DIFF_DQK) == j, q, jnp.zeros_like(q)) for j in range(2)]

    def block(off, carry, keep):
        k = k_ref[pl.ds(off, tq), :]
        v = v_ref[pl.ds(off, tq), :]
        ss = [_dot_nt(k, qs[j]) for j in range(2)]
        out = []
        for j in range(2):
            m_prev, l_prev, a_prev = carry[j]
            s = ss[j] if keep is None else jnp.where(keep, ss[j], NEG)
            m_new = jnp.maximum(m_prev, jnp.max(s, axis=0, keepdims=True))
            alpha = jnp.exp(m_prev - m_new)
            p = jnp.exp(s - m_new)
            l_new = alpha * l_prev + jnp.sum(p, axis=0, keepdims=True)
            pv = lax.dot_general(v, p.astype(BF16), (((0,), (0,)), ((), ())), preferred_element_type=F32)
            out.append((m_new, l_new, alpha * a_prev + pv))
        return tuple(out)

    one = (jnp.full((1, tq), NEG, F32), jnp.zeros((1, tq), F32), jnp.zeros((DIFF_DV, tq), F32))
    carry = lax.fori_loop(0, qi, lambda ki, c: block(pl.multiple_of(ki * tq, tq), c, None), (one, one))
    rr = lax.broadcasted_iota(jnp.int32, (tq, tq), 0)
    cc = lax.broadcasted_iota(jnp.int32, (tq, tq), 1)
    (m0, l0, a0), (m1, l1, a1) = block(pl.multiple_of(qi * tq, tq), carry, rr <= cc)
    del m0, m1
    lam = _lambda(l1q_ref[...], l1k_ref[...], l2q_ref[...], l2k_ref[...], lam_init)
    o = a0 / l0 - lam * (a1 / l1)
    o = o * lax.rsqrt(jnp.mean(o * o, axis=0, keepdims=True) + EPS) * g_ref[...]
    o_ref[...] = jnp.transpose(o * (1.0 - lam_init)).astype(o_ref.dtype)


def _flash_attention(q, k, v, lparams, g, B, T, lam_init):
    tq = min(T, 512)
    nq = T // tq
    vec64 = pl.BlockSpec((1, DIFF_DQK), lambda b_, h, qi: (0, 0))
    kv = pl.BlockSpec((T, LANES), lambda b_, h, qi: (b_, h))
    return pl.pallas_call(
        functools.partial(_flash_kernel, tq=tq, lam_init=lam_init),
        grid=(B, DIFF_HEADS, nq),
        in_specs=[pl.BlockSpec((tq, LANES), lambda b_, h, qi: (b_ * nq + qi, h)), kv, kv,
                  vec64, vec64, vec64, vec64,
                  pl.BlockSpec((DIFF_DV, 1), lambda b_, h, qi: (0, 0))],
        out_specs=pl.BlockSpec((tq, LANES), lambda b_, h, qi: (b_ * nq + qi, h)),
        out_shape=SDS((B * T, DIFF_HEADS * DIFF_DV), BF16),
        compiler_params=_cparams("parallel", "parallel", "arbitrary"),
        name="flash_diff_attention",
    )(q, k, v, *lparams, g.reshape(DIFF_DV, 1))


def _paged_kernel(pt_ref, q_ref, kn_ref, vn_ref, l1q_ref, l1k_ref, l2q_ref, l2k_ref, g_ref, e_ref, *rest,
                  pps, nsteps, tnew, lam_init):
    del pt_ref
    kv_refs = rest[:2 * pps]
    o_ref, m_sc, l_sc, acc_sc = rest[2 * pps:]
    s_id = pl.program_id(1)
    nsub = 2 * DIFF_HEADS
    nrow = nsub * tnew
    kdim = nsub * DIFF_DQK

    @pl.when(s_id == 0)
    def _():
        m_sc[...] = jnp.full(m_sc.shape, NEG, F32)
        l_sc[...] = jnp.zeros(l_sc.shape, F32)
        acc_sc[...] = jnp.zeros(acc_sc.shape, F32)

    ntok = kn_ref.shape[-1]
    er = lax.broadcasted_iota(jnp.int32, (nrow, ntok * DIFF_HEADS), 0)
    ec = lax.broadcasted_iota(jnp.int32, (nrow, ntok * DIFF_HEADS), 1)
    own_head = (ec % DIFF_HEADS) == (er // (2 * tnew))

    def update(k_refs, v_refs, keep):
        ss = [_dot(q_ref[0], k[...].reshape(kdim, ntok).astype(BF16)) for k in k_refs]
        if keep is not None:
            ss = [jnp.where(keep, s, NEG) for s in ss]
        m_prev = m_sc[...]
        m_new = m_prev
        for s in ss:
            m_new = jnp.maximum(m_new, jnp.max(s, axis=-1, keepdims=True))
        alpha = jnp.exp(m_prev - m_new)
        l_new = alpha * l_sc[...]
        acc = alpha * acc_sc[...]
        for s, v in zip(ss, v_refs):
            p = jnp.exp(s - m_new)
            if keep is not None:
                p = jnp.where(keep, p, 0.0)
            l_new = l_new + jnp.sum(p, axis=-1, keepdims=True)
            p_wide = _dot(p.astype(BF16), e_ref[...])
            p_bd = jnp.where(own_head, p_wide, 0.0).astype(BF16)
            acc = acc + _dot(p_bd, v[...].reshape(ntok * DIFF_HEADS, DIFF_DV).astype(BF16))
        l_sc[...] = l_new
        acc_sc[...] = acc
        m_sc[...] = m_new

    @pl.when(s_id < nsteps - 1)
    def _():
        update(kv_refs[:pps], kv_refs[pps:], None)

    @pl.when(s_id == nsteps - 1)
    def _():
        rr = lax.broadcasted_iota(jnp.int32, (nrow, ntok), 0)
        cc = lax.broadcasted_iota(jnp.int32, (nrow, ntok), 1)
        update([kn_ref.at[0]], [vn_ref.at[0]], cc <= (rr % tnew))
        lam = _lambda(l1q_ref[...], l1k_ref[...], l2q_ref[...], l2k_ref[...], lam_init)
        o = acc_sc[...] / l_sc[...]
        rows = 2 * tnew
        for hv in range(DIFF_HEADS):
            o1 = o[hv * rows:hv * rows + tnew]
            o2 = o[hv * rows + tnew:(hv + 1) * rows]
            d = o1 - lam * o2
            d = d * lax.rsqrt(jnp.mean(d * d, axis=-1, keepdims=True) + EPS) * g_ref[...]
            o_ref[0, hv * tnew:(hv + 1) * tnew, :] = d * (1.0 - lam_init)


def _paged_attention(q_bd, k_new_t, v_new, cache_kt, cache_v, page_table, layer, lparams, g, lam_init, tnew):
    DB, n_pages = page_table.shape
    page = cache_v.shape[2]
    pps = 8 if n_pages % 8 == 0 else 1
    nsteps = n_pages // pps + 1
    nsub = 2 * DIFF_HEADS
    nrow = nsub * tnew

    def kv_spec(r, shape):
        def imap(b_, s, pt):
            idx = jnp.minimum(s * pps + r, n_pages - 1)
            return (layer, pt[b_ * n_pages + idx], 0, 0, 0)
        return pl.BlockSpec((None, None) + shape, imap)

    vec64 = pl.BlockSpec((1, DIFF_DQK), lambda b_, s, pt: (0, 0))
    in_specs = [pl.BlockSpec((1, nrow, nsub * DIFF_DQK), lambda b_, s, pt: (b_, 0, 0)),
                pl.BlockSpec((1, nsub, DIFF_DQK, page), lambda b_, s, pt: (b_, 0, 0, 0)),
                pl.BlockSpec((1, page, DIFF_HEADS, DIFF_DV), lambda b_, s, pt: (b_, 0, 0, 0)),
                vec64, vec64, vec64, vec64,
                pl.BlockSpec((1, DIFF_DV), lambda b_, s, pt: (0, 0)),
                pl.BlockSpec((page, page * DIFF_HEADS), lambda b_, s, pt: (0, 0))]
    spread = (jnp.arange(page * DIFF_HEADS)[None, :] // DIFF_HEADS == jnp.arange(page)[:, None]).astype(BF16)
    in_specs += [kv_spec(r, (nsub, DIFF_DQK, page)) for r in range(pps)]
    in_specs += [kv_spec(r, (page, DIFF_HEADS, DIFF_DV)) for r in range(pps)]
    grid_spec = pltpu.PrefetchScalarGridSpec(
        num_scalar_prefetch=1,
        grid=(DB, nsteps),
        in_specs=in_specs,
        out_specs=pl.BlockSpec((1, DIFF_HEADS * tnew, DIFF_DV), lambda b_, s, pt: (b_, 0, 0)),
        scratch_shapes=[pltpu.VMEM((nrow, 1), F32), pltpu.VMEM((nrow, 1), F32),
                        pltpu.VMEM((nrow, DIFF_DV), F32)])
    return pl.pallas_call(
        functools.partial(_paged_kernel, pps=pps, nsteps=nsteps, tnew=tnew, lam_init=lam_init),
        grid_spec=grid_spec,
        out_shape=SDS((DB, DIFF_HEADS * tnew, DIFF_DV), F32),
        compiler_params=_cparams("parallel", "arbitrary"),
        name="paged_diff_attention",
    )(page_table.reshape(-1), q_bd, k_new_t, v_new, *lparams, g, spread,
      *([cache_kt] * pps), *([cache_v] * pps))


def _outproj_kernel(a_ref, b_ref, c_ref, wa_ref, wb_ref, wc_ref, x_ref, o_ref):
    acc = _dot(a_ref[...].astype(BF16), wa_ref[...])
    acc = acc + _dot(b_ref[...].astype(BF16), wb_ref[...])
    acc = acc + _dot(c_ref[...].astype(BF16), wc_ref[...])
    o_ref[...] = x_ref[...] + acc


def _out_proj(conf, odn, att, w_out, x):
    n = x.shape[0]
    tm = min(n, 1024)
    tn = 512
    return pl.pallas_call(
        _outproj_kernel,
        grid=(n // tm, D_MODEL // tn),
        in_specs=[pl.BlockSpec((tm, 512), lambda i, j: (i, 0)),
                  pl.BlockSpec((tm, 512), lambda i, j: (i, 0)),
                  pl.BlockSpec((tm, 1024), lambda i, j: (i, 0)),
                  pl.BlockSpec((512, tn), lambda i, j: (0, j)),
                  pl.BlockSpec((512, tn), lambda i, j: (1, j)),
                  pl.BlockSpec((1024, tn), lambda i, j: (1, j)),
                  pl.BlockSpec((tm, tn), lambda i, j: (i, j))],
        out_specs=pl.BlockSpec((tm, tn), lambda i, j: (i, j)),
        out_shape=SDS((n, D_MODEL), F32),
        compiler_params=_cparams("parallel", "arbitrary"),
        name="out_proj",
    )(conf, odn, att, w_out, w_out, w_out, x)


META_GROUP = N_EXPERTS
META_RANK = N_EXPERTS + 1
ROUTED_W = D_MODEL + LANES


def _router_kernel(x_ref, g_ref, wr_ref, br_ref, hg_ref, cnt_ref, carry):
    i = pl.program_id(0)

    @pl.when(i == 0)
    def _():
        carry[...] = jnp.zeros(carry.shape, F32)

    x = x_ref[...]
    tm = x.shape[0]
    y = x * lax.rsqrt(jnp.mean(x * x, axis=-1, keepdims=True) + EPS)
    h = y * g_ref[...]
    hg_ref[:, :D_MODEL] = h
    logits = _mm3(h, wr_ref[...]) + br_ref[...]
    lane = lax.broadcasted_iota(jnp.int32, logits.shape, 1)
    big = jnp.int32(1 << 20)
    is_g = (lane >= N_EXPERTS) & (lane < N_EXPERTS + N_GROUPS)
    gl = jnp.where(is_g, logits, NEG)
    gmax = jnp.max(gl, axis=-1, keepdims=True)
    gsel = jnp.min(jnp.where(is_g & (gl == gmax), lane, big), axis=-1, keepdims=True) - N_EXPERTS
    gw = 1.0 / jnp.sum(jnp.where(is_g, jnp.exp(gl - gmax), 0.0), axis=-1, keepdims=True)
    in_grp = (lane >= gsel * EXPERTS_PER_GROUP) & (lane < (gsel + 1) * EXPERTS_PER_GROUP)
    el = jnp.where(in_grp, logits, NEG)
    v1 = jnp.max(el, axis=-1, keepdims=True)
    i1 = jnp.min(jnp.where(in_grp & (el == v1), lane, big), axis=-1, keepdims=True)
    rest = in_grp & (lane != i1)
    el2 = jnp.where(rest, logits, NEG)
    v2 = jnp.max(el2, axis=-1, keepdims=True)
    i2 = jnp.min(jnp.where(rest & (el2 == v2), lane, big), axis=-1, keepdims=True)
    e2 = jnp.exp(v2 - v1)
    den = 1.0 + e2
    gates = jnp.where(lane == i1, (1.0 / den) * gw, 0.0) + jnp.where(lane == i2, (e2 / den) * gw, 0.0)
    onehot = jnp.where(lane == gsel, 1.0, 0.0)
    rr = lax.broadcasted_iota(jnp.int32, (tm, tm), 0)
    cc = lax.broadcasted_iota(jnp.int32, (tm, tm), 1)
    before = _dot(jnp.where(cc < rr, 1.0, 0.0).astype(BF16), onehot.astype(BF16)) + carry[...]
    rank = jnp.sum(onehot * before, axis=-1, keepdims=True)
    hg_ref[:, D_MODEL:] = (gates + jnp.where(lane == META_GROUP, gsel.astype(F32), 0.0)
                           + jnp.where(lane == META_RANK, rank, 0.0))
    carry[...] = carry[...] + jnp.sum(onehot, axis=0, keepdims=True)
    cnt_ref[...] = carry[...]


def _router(x, g, wr, br):
    n = x.shape[0]
    tm = min(n, 512)
    return pl.pallas_call(
        _router_kernel,
        grid=(n // tm,),
        in_specs=[pl.BlockSpec((tm, D_MODEL), lambda i: (i, 0)),
                  pl.BlockSpec((1, D_MODEL), lambda i: (0, 0)),
                  pl.BlockSpec((D_MODEL, LANES), lambda i: (0, 0)),
                  pl.BlockSpec((1, LANES), lambda i: (0, 0))],
        out_specs=[pl.BlockSpec((tm, ROUTED_W), lambda i: (i, 0)),
                   pl.BlockSpec((1, LANES), lambda i: (0, 0))],
        out_shape=[SDS((n, ROUTED_W), F32), SDS((1, LANES), F32)],
        scratch_shapes=[pltpu.VMEM((1, LANES), F32)],
        compiler_params=_cparams("arbitrary"),
        name="norm_router",
    )(x, g, wr, br)


def _route_plan(hg, counts, tile, n_tiles):
    cnt = counts[0, :N_GROUPS].astype(jnp.int32)
    size = ((cnt + tile - 1) // tile) * tile
    start = jnp.cumsum(size) - size
    grp = hg[:, D_MODEL + META_GROUP].astype(jnp.int32)
    rank = hg[:, D_MODEL + META_RANK].astype(jnp.int32)
    dest = rank
    for k in range(N_GROUPS):
        dest = dest + jnp.where(grp == k, start[k], 0)
    tile_start = jnp.arange(n_tiles, dtype=jnp.int32) * tile
    tile_group = jnp.sum((tile_start[:, None] >= start[None, :]).astype(jnp.int32), axis=1) - 1
    n_used = (jnp.sum(size) // tile).astype(jnp.int32).reshape(1)
    return dest, jnp.clip(tile_group, 0, N_GROUPS - 1), n_used


def _row_copy(src_ref, src_row, dst_ref, dst_row, sem):
    return pltpu.make_async_copy(src_ref.at[pl.ds(src_row, 1), :], dst_ref.at[pl.ds(dst_row, 1), :], sem)


def _dispatch_kernel(dest_ref, hg_ref, xs_in_ref, xs_ref, sem, *, rows):
    del xs_in_ref
    base = pl.program_id(0) * rows

    def issue(r, c):
        _row_copy(hg_ref, r, xs_ref, dest_ref[base + r], sem).start()
        return c

    def drain(r, c):
        _row_copy(hg_ref, 0, xs_ref, 0, sem).wait()
        return c

    lax.fori_loop(0, rows, issue, 0, unroll=8)
    lax.fori_loop(0, rows, drain, 0, unroll=8)


def _dispatch(dest, hg, n_rows):
    n = hg.shape[0]
    rows = min(n, 512)
    grid_spec = pltpu.PrefetchScalarGridSpec(
        num_scalar_prefetch=1, grid=(n // rows,),
        in_specs=[pl.BlockSpec((rows, ROUTED_W), lambda i, d: (i, 0)),
                  pl.BlockSpec(memory_space=pl.ANY)],
        out_specs=pl.BlockSpec(memory_space=pl.ANY),
        scratch_shapes=[pltpu.SemaphoreType.DMA(())])
    return pl.pallas_call(
        functools.partial(_dispatch_kernel, rows=rows),
        grid_spec=grid_spec,
        out_shape=SDS((n_rows, ROUTED_W), F32),
        input_output_aliases={2: 0},
        compiler_params=_cparams("arbitrary"),
        name="moe_dispatch",
    )(dest, hg, jnp.zeros((n_rows, ROUTED_W), F32))


def _experts_kernel(tg_ref, nu_ref, xs_ref, wg_ref, wu_ref, wd_ref, o_ref):
    i = pl.program_id(0)
    e = pl.program_id(1)

    @pl.when(e == 0)
    def _():
        o_ref[...] = jnp.zeros(o_ref.shape, F32)

    @pl.when(i < nu_ref[0])
    def _():
        h = xs_ref[:, :D_MODEL].astype(BF16)
        meta = xs_ref[:, D_MODEL:]
        lane = lax.broadcasted_iota(jnp.int32, meta.shape, 1)
        col = tg_ref[i] * EXPERTS_PER_GROUP + e
        ge = jnp.sum(jnp.where(lane == col, meta, 0.0), axis=-1, keepdims=True)
        act = _silu(_dot(h, wg_ref[...])) * _dot(h, wu_ref[...]) * ge
        o_ref[...] += _dot(act.astype(BF16), wd_ref[...])


def _experts(tile_group, n_used, xs, wg, wu, wd, layer, tile):
    n_rows = xs.shape[0]
    wspec = lambda shape: pl.BlockSpec(
        (None, None) + shape, lambda i, e, tg, nu: (layer, tg[i] * EXPERTS_PER_GROUP + e, 0, 0))
    grid_spec = pltpu.PrefetchScalarGridSpec(
        num_scalar_prefetch=2, grid=(n_rows // tile, EXPERTS_PER_GROUP),
        in_specs=[pl.BlockSpec((tile, ROUTED_W), lambda i, e, tg, nu: (i, 0)),
                  wspec((D_MODEL, D_EXPERT)), wspec((D_MODEL, D_EXPERT)), wspec((D_EXPERT, D_MODEL))],
        out_specs=pl.BlockSpec((tile, D_MODEL), lambda i, e, tg, nu: (i, 0)))
    return pl.pallas_call(
        _experts_kernel,
        grid_spec=grid_spec,
        out_shape=SDS((n_rows, D_MODEL), F32),
        compiler_params=_cparams("parallel", "arbitrary"),
        name="moe_experts",
    )(tile_group, n_used, xs, wg, wu, wd)


def _combine_kernel(dest_ref, x_ref, ys_ref, o_ref, buf, sem, *, rows):
    base = pl.program_id(0) * rows

    def issue(r, c):
        _row_copy(ys_ref, dest_ref[base + r], buf, r, sem).start()
        return c

    def drain(r, c):
        _row_copy(ys_ref, 0, buf, 0, sem).wait()
        return c

    lax.fori_loop(0, rows, issue, 0, unroll=8)
    lax.fori_loop(0, rows, drain, 0, unroll=8)
    o_ref[...] = x_ref[...] + buf[...]


def _combine(dest, x, ys):
    n = x.shape[0]
    rows = min(n, 512)
    grid_spec = pltpu.PrefetchScalarGridSpec(
        num_scalar_prefetch=1, grid=(n // rows,),
        in_specs=[pl.BlockSpec((rows, D_MODEL), lambda i, d: (i, 0)),
                  pl.BlockSpec(memory_space=pl.ANY)],
        out_specs=pl.BlockSpec((rows, D_MODEL), lambda i, d: (i, 0)),
        scratch_shapes=[pltpu.VMEM((rows, D_MODEL), F32), pltpu.SemaphoreType.DMA(())])
    return pl.pallas_call(
        functools.partial(_combine_kernel, rows=rows),
        grid_spec=grid_spec,
        out_shape=SDS((n, D_MODEL), F32),
        compiler_params=_cparams("arbitrary"),
        name="moe_combine",
    )(dest, x, ys)


def _moe_block(x, g, wr, br, wg, wu, wd, layer):
    n = x.shape[0]
    tile = min(n, 512)
    n_tiles = n // tile + N_GROUPS
    hg, counts = _router(x, g, wr, br)
    dest, tile_group, n_used = _route_plan(hg, counts, tile, n_tiles)
    xs = _dispatch(dest, hg, n_tiles * tile)
    ys = _experts(tile_group, n_used, xs, wg, wu, wd, layer, tile)
    return _combine(dest, x, ys)


def _row(v, width=None):
    v = v.astype(F32).reshape(1, -1)
    if width is not None and v.shape[1] < width:
        v = jnp.pad(v, ((0, 0), (0, width - v.shape[1])))
    return v


def _prep_layer(l, w):
    wi = w['w_in'][l]
    w_main = jnp.concatenate([wi[:, 1024:2560], wi[:, 2560:3072], wi[:, 0:1024], wi[:, 3080:6152]],
                             axis=1).astype(BF16)
    w_ba = jnp.pad(wi[:, 3072:3080], ((0, 0), (0, LANES - 8))).astype(BF16)
    wr = jnp.concatenate([w['router_expert_w'][l], w['router_group_w'][l]], axis=1)
    wr = jnp.pad(wr, ((0, 0), (0, LANES - wr.shape[1])))
    br = _row(jnp.concatenate([w['router_expert_b'][l], w['router_group_b'][l]]), LANES)
    lane_pad = lambda v: _row(jnp.concatenate([jnp.zeros((DN_HEADS,), F32), v.astype(F32)]), LANES)
    return dict(
        norm1_g=_row(w['norm1_g'][l]), w_main=w_main, w_ba=w_ba,
        conf_w=w['conf_dw_w'][l], conf_b=_row(w['conf_dw_b'][l]),
        conf_g=_row(w['conf_ln_g'][l]), conf_bb=_row(w['conf_ln_b'][l]),
        dn_cw=w['dn_conv_w'][l], dn_av=lane_pad(w['dn_a_log'][l]), dn_dv=lane_pad(w['dn_dt_bias'][l]),
        dn_ng=_row(w['dn_norm_g'][l]),
        gq=_row(jnp.tile(w['diff_q_norm_g'][l], 2)), gk=_row(jnp.tile(w['diff_k_norm_g'][l], 2)),
        lparams=tuple(_row(w[k][l]) for k in ('diff_lambda_q1', 'diff_lambda_k1',
                                               'diff_lambda_q2', 'diff_lambda_k2')),
        subln=_row(w['diff_subln_g'][l]),
        w_out=w['w_out'][l].astype(BF16), norm2_g=_row(w['norm2_g'][l]), wr=wr, br=br,
        layer=l)


def _layer(x, B, T, tables, conf_prev, dnc_prev, dn_s0, lp, moe_w, lam_init, paged):
    n = B * T
    act_dtype = BF16 if T % 16 == 0 else F32
    p, ba = _in_proj(x, lp['norm1_g'], lp['w_main'], lp['w_ba'])
    conf, conf_new = _conformer(p, conf_prev, lp['conf_w'], lp['conf_b'], lp['conf_g'], lp['conf_bb'],
                                B, T, act_dtype)
    odn, dnc_new, dn_state = _deltanet(p, ba, dnc_prev, lp['dn_cw'], lp['dn_av'], lp['dn_dv'],
                                       lp['dn_ng'], dn_s0, B, T, act_dtype)
    qn, kf, kb, vb = _qknorm(p, lp['gq'], lp['gk'], *tables, n, T, act_dtype)
    v_f32 = p[:, P_DV:P_DV + DIFF_HEADS * DIFF_DV]
    k_out = kf.reshape(B, T, 2 * DIFF_HEADS, DIFF_DQK)
    v_out = v_f32.reshape(B, T, DIFF_HEADS, DIFF_DV)
    if paged is None:
        att = _flash_attention(qn, kb, vb, lp['lparams'], lp['subln'], B, T, lam_init)
    else:
        cache_kt, cache_v, page_table, layer = paged
        page = cache_v.shape[2]
        nsub = 2 * DIFF_HEADS
        q4 = qn.reshape(B, T, nsub, DIFF_DQK).transpose(0, 2, 1, 3)
        eye = jnp.eye(nsub, dtype=qn.dtype)
        q_bd = (q4[:, :, :, None, :] * eye[None, :, None, :, None]).reshape(B, nsub * T, nsub * DIFF_DQK)
        k_new_t = jnp.pad(k_out.transpose(0, 2, 3, 1), ((0, 0), (0, 0), (0, 0), (0, page - T)))
        v_new = jnp.pad(v_out, ((0, 0), (0, page - T), (0, 0), (0, 0)))
        att = _paged_attention(q_bd.astype(BF16), k_new_t, v_new, cache_kt, cache_v, page_table, layer,
                               lp['lparams'], lp['subln'], lam_init, T)
        att = att.reshape(B, DIFF_HEADS, T, DIFF_DV).transpose(0, 2, 1, 3).reshape(n, DIFF_HEADS * DIFF_DV)
    x2 = _out_proj(conf, odn, att, lp['w_out'], x)
    x3 = _moe_block(x2, lp['norm2_g'], lp['wr'], lp['br'], *moe_w, lp['layer'])
    return x3, (k_out, v_out, dn_state, dnc_new, conf_new)


def kernel(x_prompt, x_sample, cache_k, cache_v, page_table, state_delta, state_delta_conv, state_conf_conv, norm1_g, w_in, conf_dw_w, conf_dw_b, conf_ln_g, conf_ln_b, dn_conv_w, dn_a_log, dn_dt_bias, dn_norm_g, diff_q_norm_g, diff_k_norm_g, diff_lambda_q1, diff_lambda_k1, diff_lambda_q2, diff_lambda_k2, diff_subln_g, w_out, norm2_g, router_group_w, router_group_b, router_expert_w, router_expert_b, moe_w_gate, moe_w_up, moe_w_down):
    w = dict(norm1_g=norm1_g, w_in=w_in, conf_dw_w=conf_dw_w, conf_dw_b=conf_dw_b, conf_ln_g=conf_ln_g,
             conf_ln_b=conf_ln_b, dn_conv_w=dn_conv_w, dn_a_log=dn_a_log, dn_dt_bias=dn_dt_bias,
             dn_norm_g=dn_norm_g, diff_q_norm_g=diff_q_norm_g, diff_k_norm_g=diff_k_norm_g,
             diff_lambda_q1=diff_lambda_q1, diff_lambda_k1=diff_lambda_k1, diff_lambda_q2=diff_lambda_q2,
             diff_lambda_k2=diff_lambda_k2, diff_subln_g=diff_subln_g, w_out=w_out, norm2_g=norm2_g,
             router_group_w=router_group_w, router_group_b=router_group_b,
             router_expert_w=router_expert_w, router_expert_b=router_expert_b,
             moe_w_gate=moe_w_gate, moe_w_up=moe_w_up, moe_w_down=moe_w_down)
    B, S, _ = x_prompt.shape
    DB, T, _ = x_sample.shape
    depth = w_in.shape[0]
    past_len = page_table.shape[1] * cache_k.shape[2]
    cache_kt = jnp.transpose(cache_k, (0, 1, 3, 4, 2))
    tab_p = _rope_tables(jnp.arange(S, dtype=jnp.int32))
    tab_s = tuple(jnp.tile(a, (DB, 1)) for a in _rope_tables(past_len + jnp.arange(T, dtype=jnp.int32)))
    hp = x_prompt.reshape(B * S, D_MODEL)
    hs = x_sample.reshape(DB * T, D_MODEL)
    zero_conf = jnp.zeros((B, CONF_K - 1, CONF_W), F32)
    zero_dnc = jnp.zeros((B, DN_CONV - 1, DN_QKV), F32)
    zero_s = jnp.zeros((B, DN_HEADS, DN_DK, DN_DV), F32)
    moe_w = (moe_w_gate.astype(BF16), moe_w_up.astype(BF16), moe_w_down.astype(BF16))
    outs_p, outs_s = [], []
    for l in range(depth):
        lp = _prep_layer(l, w)
        lam_init = 0.8 - 0.6 * math.exp(-0.3 * l)
        hp, sp = _layer(hp, B, S, tab_p, zero_conf, zero_dnc, zero_s, lp, moe_w, lam_init, None)
        hs, ss = _layer(hs, DB, T, tab_s, state_conf_conv[l], state_delta_conv[l], state_delta[l], lp,
                        moe_w, lam_init, (cache_kt, cache_v, page_table, l))
        outs_p.append(sp)
        outs_s.append(ss)
    st = lambda outs, i: jnp.stack([o[i] for o in outs])
    return (hp.reshape(B, S, D_MODEL), hs.reshape(DB, T, D_MODEL),
            st(outs_p, 0), st(outs_p, 1), st(outs_s, 0), st(outs_s, 1),
            st(outs_p, 2), st(outs_s, 2), st(outs_p, 3), st(outs_s, 3),
            st(outs_p, 4), st(outs_s, 4))
```
